```python
import math
import jax, jax.numpy as jnp
from jax import lax
import numpy as np

D_MODEL = 4096
BATCH = 4
SEQ = 4096
DEPTH = 2

MIX_WIDTH = D_MODEL
NORM_EPS = 1e-5
SB_HEAD_DIM = 128
SB_WIDTH = MIX_WIDTH // 4
SB_HEADS = SB_WIDTH // SB_HEAD_DIM
SB_BLOCK = 128
HG_EXPAND = 128
HG_WIDTH = MIX_WIDTH // 4
HG_HEADS = HG_WIDTH // HG_EXPAND
HG_HEAD_V = HG_WIDTH // HG_HEADS
HG_CHUNK = 64
HG_MIN_F = 1e-30
RW_HEAD_DIM = 64
RW_WIDTH = MIX_WIDTH - SB_WIDTH - HG_WIDTH
RW_HEADS = RW_WIDTH // RW_HEAD_DIM
RW_DECAY_RANK = max(32, int(round(math.sqrt(RW_WIDTH) * 1.8 / 32)) * 32)
RW_AAA_RANK = max(32, int(round(math.sqrt(RW_WIDTH) * 1.8 / 32)) * 32)
RW_GATE_RANK = max(32, int(round((RW_WIDTH ** 0.6) * 0.8 / 32)) * 32)
RW_GN_EPS = 64e-5
RW_COLS = 3 * RW_WIDTH + RW_DECAY_RANK + RW_AAA_RANK + RW_GATE_RANK
D_FF = 4 * D_MODEL
IN_COLS = 3 * SB_WIDTH + 4 * HG_WIDTH + RW_COLS
IN_SPLITS = [SB_WIDTH, 2 * SB_WIDTH, 3 * SB_WIDTH,
             3 * SB_WIDTH + HG_WIDTH, 3 * SB_WIDTH + 2 * HG_WIDTH,
             3 * SB_WIDTH + 3 * HG_WIDTH, 3 * SB_WIDTH + 4 * HG_WIDTH]
RW_SPLITS = [RW_WIDTH, 2 * RW_WIDTH, 3 * RW_WIDTH,
             3 * RW_WIDTH + RW_DECAY_RANK, 3 * RW_WIDTH + RW_DECAY_RANK + RW_AAA_RANK]

kernel_name = "hymba_style_sb_hgrn2_rwkv7_hybrid"


def rms_norm(x, g, eps=NORM_EPS):
    xf = x.astype(jnp.float32)
    y = xf * lax.rsqrt(jnp.mean(xf * xf, axis=-1, keepdims=True) + eps)
    return (y * g.astype(jnp.float32)).astype(x.dtype)


def stick_breaking_attention(q, k, v):
    T = q.shape[1]
    qh = jnp.swapaxes(q, 1, 2).astype(jnp.float32)
    kh = jnp.swapaxes(k, 1, 2).astype(jnp.float32)
    vh = jnp.swapaxes(v, 1, 2).astype(jnp.float32)
    scale = SB_HEAD_DIM ** -0.5
    outs = []
    for blk in range(T // SB_BLOCK):
        start = blk * SB_BLOCK
        stop = start + SB_BLOCK
        z = jnp.einsum('bhqd,bhkd->bhqk', qh[:, :, start:stop], kh[:, :, :stop]) * scale
        t_pos = start + jnp.arange(SB_BLOCK)[:, None]
        s_pos = jnp.arange(stop)[None, :]
        before = s_pos < t_pos
        log_keep = jnp.where(before, jax.nn.log_sigmoid(-z), 0.0)
        log_between = lax.cumsum(log_keep, axis=3, reverse=True) - log_keep
        log_w = jnp.where(before, jax.nn.log_sigmoid(z) + log_between, 0.0)
        weight = jnp.where(before, jnp.exp(log_w), 0.0)
        outs.append(jnp.einsum('bhqk,bhkd->bhqd', weight, vh[:, :, :stop]))
    o = jnp.concatenate(outs, axis=2)
    return jnp.swapaxes(o, 1, 2)


def hgrn2_chunked(q, k, v, log_f):
    B, H, T, K = q.shape
    V = v.shape[-1]
    n_chunks = T // HG_CHUNK

    def chunks(a):
        return jnp.moveaxis(a.reshape(B, H, n_chunks, HG_CHUNK, a.shape[-1]), 2, 0)

    causal = jnp.tril(jnp.ones((HG_CHUNK, HG_CHUNK), dtype=bool))[:, :, None]

    def step(state, inp):
        qc, kc, vc, gc = inp
        b = jnp.cumsum(gc, axis=2)
        b_last = b[:, :, -1:, :]
        o_inter = jnp.einsum('bhtk,bhkv->bhtv', qc * jnp.exp(b), state)
        diff = b[:, :, :, None, :] - b[:, :, None, :, :]
        decay = jnp.where(causal, jnp.exp(jnp.where(causal, diff, 0.0)), 0.0)
        scores = jnp.einsum('bhtk,bhsk,bhtsk->bhts', qc, kc, decay)
        o_intra = jnp.einsum('bhts,bhsv->bhtv', scores, vc)
        state = (jnp.exp(b_last[:, :, 0, :])[..., None] * state
                 + jnp.einsum('bhsk,bhsv->bhkv', kc * jnp.exp(b_last - b), vc))
        return state, o_inter + o_intra

    state0 = jnp.zeros((B, H, K, V), jnp.float32)
    _, o = lax.scan(step, state0, (chunks(q), chunks(k), chunks(v), chunks(log_f)))
    return jnp.moveaxis(o, 0, 2).reshape(B, H, T, V)


def hgrn2_mix(hq, hf, hi, hg, lb, norm_g):
    B, T, _ = hq.shape

    def heads(a):
        return jnp.swapaxes(a.reshape(B, T, HG_HEADS, -1), 1, 2).astype(jnp.float32)

    fr = hf.astype(jnp.float32)
    f = lb + (1.0 - lb) * jax.nn.sigmoid(fr)
    log_f = jnp.log(jnp.maximum(f, HG_MIN_F))
    key = (1.0 - lb) * jax.nn.sigmoid(-fr)
    o = hgrn2_chunked(heads(hq), heads(key), heads(hi), heads(log_f))
    o = jnp.swapaxes(o, 1, 2)
    o = o * lax.rsqrt(jnp.mean(o * o, axis=-1, keepdims=True) + NORM_EPS) * norm_g
    return o.reshape(B, T, HG_WIDTH) * jax.nn.silu(hg.astype(jnp.float32))


def rwkv7_recurrence(r, decay, k, v, kk, a):
    B, T, H, N = r.shape

    def step(state, inp):
        r_t, w_t, k_t, v_t, kk_t, a_t = inp
        sa = jnp.einsum('bhvk,bhk->bhv', state, -kk_t)
        state = (state * w_t[:, :, None, :]
                 + sa[..., None] * (kk_t * a_t)[:, :, None, :]
                 + v_t[..., None] * k_t[:, :, None, :])
        return state, jnp.einsum('bhvk,bhk->bhv', state, r_t)

    xs = tuple(jnp.moveaxis(t, 1, 0) for t in (r, decay, k, v, kk, a))
    _, y = lax.scan(step, jnp.zeros((B, H, N, N), jnp.float32), xs)
    return jnp.moveaxis(y, 0, 1)


def rwkv7_mix(rw, mu, w0, w_w2, a0, w_a2, w_g2, k_k, k_a, r_k, lnx_w, lnx_b):
    B, T, _ = rw.shape
    rwf = rw.astype(jnp.float32)
    prev = jnp.pad(rwf, ((0, 0), (1, 0), (0, 0)))[:, :-1]
    rwf = rwf + mu * (prev - rwf)
    r, k, v, w_in, a_in, g_in = jnp.split(rwf, RW_SPLITS, axis=-1)
    w_log = -jax.nn.softplus(-(w0 + jnp.tanh(w_in) @ w_w2)) - 0.5
    decay = jnp.exp(-jnp.exp(w_log))
    a = jax.nn.sigmoid(a0 + a_in @ w_a2)
    g = jax.nn.sigmoid(g_in) @ w_g2

    def heads(t):
        return t.reshape(B, T, RW_HEADS, RW_HEAD_DIM)

    kk = heads(k * k_k)
    kk = kk * lax.rsqrt(jnp.maximum(jnp.sum(kk * kk, axis=-1, keepdims=True), 1e-24))
    k = k * (1.0 + (a - 1.0) * k_a)
    rh, kh, vh = heads(r), heads(k), heads(v)
    y = rwkv7_recurrence(rh, heads(decay), kh, vh, kk, heads(a))
    mean = jnp.mean(y, axis=-1, keepdims=True)
    var = jnp.mean(jnp.square(y - mean), axis=-1, keepdims=True)
    y = ((y - mean) * lax.rsqrt(var + RW_GN_EPS)).reshape(B, T, RW_WIDTH) * lnx_w + lnx_b
    bonus = jnp.sum(rh * kh * r_k, axis=-1, keepdims=True) * vh
    return (y + bonus.reshape(B, T, RW_WIDTH)) * g


def setup_inputs(seed: int = 0) -> dict:
    key = jax.random.key(seed)
    ks = jax.random.split(key, 24)
    f32 = jnp.float32
    nrm = lambda k, shape: jax.random.normal(k, shape, f32)
    return {
        "x": nrm(ks[0], (BATCH, SEQ, D_MODEL)),
        "norm1_g": 1.0 + 0.02 * nrm(ks[1], (DEPTH, D_MODEL)),
        "w_in": nrm(ks[2], (DEPTH, D_MODEL, IN_COLS)) * D_MODEL ** -0.5,
        "sb_norm_g": 1.0 + 0.02 * nrm(ks[3], (DEPTH, SB_WIDTH)),
        "hg_lb_param": 0.1 * nrm(ks[4], (DEPTH, HG_WIDTH)),
        "hg_norm_g": 1.0 + 0.02 * nrm(ks[5], (DEPTH, HG_HEAD_V)),
        "rw_mu": jax.random.uniform(ks[6], (DEPTH, RW_COLS), f32, 0.1, 0.9),
        "rw_w0": jax.random.uniform(ks[7], (DEPTH, RW_WIDTH), f32, -5.0, 0.0),
        "rw_w_w2": nrm(ks[8], (DEPTH, RW_DECAY_RANK, RW_WIDTH)) * RW_DECAY_RANK ** -0.5,
        "rw_a0": 0.1 * nrm(ks[9], (DEPTH, RW_WIDTH)),
        "rw_w_a2": nrm(ks[10], (DEPTH, RW_AAA_RANK, RW_WIDTH)) * RW_AAA_RANK ** -0.5,
        "rw_w_g2": nrm(ks[11], (DEPTH, RW_GATE_RANK, RW_WIDTH)) * RW_GATE_RANK ** -0.5,
        "rw_k_k": 0.85 + 0.05 * nrm(ks[12], (DEPTH, RW_WIDTH)),
        "rw_k_a": 1.0 + 0.05 * nrm(ks[13], (DEPTH, RW_WIDTH)),
        "rw_r_k": 0.1 * nrm(ks[14], (DEPTH, RW_HEADS, RW_HEAD_DIM)),
        "rw_lnx_w": 1.0 + 0.02 * nrm(ks[15], (DEPTH, RW_WIDTH)),
        "rw_lnx_b": 0.02 * nrm(ks[16], (DEPTH, RW_WIDTH)),
        "w_out": nrm(ks[17], (DEPTH, MIX_WIDTH, D_MODEL)) * MIX_WIDTH ** -0.5,
        "norm2_g": 1.0 + 0.02 * nrm(ks[18], (DEPTH, D_MODEL)),
        "w_ff_in": nrm(ks[19], (DEPTH, D_MODEL, D_FF)) * D_MODEL ** -0.5,
        "w_ff_out": nrm(ks[20], (DEPTH, D_FF, D_MODEL)) * D_FF ** -0.5,
        "final_g": 1.0 + 0.02 * nrm(ks[21], (D_MODEL,)),
    }


def reference(x, norm1_g, w_in, sb_norm_g, hg_lb_param, hg_norm_g, rw_mu, rw_w0, rw_w_w2,
              rw_a0, rw_w_a2, rw_w_g2, rw_k_k, rw_k_a, rw_r_k, rw_lnx_w, rw_lnx_b,
              w_out, norm2_g, w_ff_in, w_ff_out, final_g):
    B, T, _ = x.shape
    probs = jax.nn.softmax(hg_lb_param.astype(jnp.float32), axis=0)
    lower_bounds = jnp.cumsum(probs, axis=0) - probs[0:1]
    for l in range(DEPTH):
        h = rms_norm(x, norm1_g[l])
        proj = h @ w_in[l]
        sb_q, sb_k, sb_v, hg_q, hg_f, hg_i, hg_g, rw = jnp.split(proj, IN_SPLITS, axis=-1)
        o_sb = stick_breaking_attention(sb_q.reshape(B, T, SB_HEADS, SB_HEAD_DIM),
                                        sb_k.reshape(B, T, SB_HEADS, SB_HEAD_DIM),
                                        sb_v.reshape(B, T, SB_HEADS, SB_HEAD_DIM))
        o_sb = o_sb * lax.rsqrt(jnp.mean(o_sb * o_sb, axis=-1, keepdims=True) + NORM_EPS)
        o_sb = o_sb.reshape(B, T, SB_WIDTH) * sb_norm_g[l]
        o_hg = hgrn2_mix(hg_q, hg_f, hg_i, hg_g, lower_bounds[l], hg_norm_g[l])
        o_rw = rwkv7_mix(rw, rw_mu[l], rw_w0[l], rw_w_w2[l], rw_a0[l], rw_w_a2[l], rw_w_g2[l],
                         rw_k_k[l], rw_k_a[l], rw_r_k[l], rw_lnx_w[l], rw_lnx_b[l])
        mix = jnp.concatenate([o_sb, o_hg, o_rw], axis=-1).astype(x.dtype)
        x = x + mix @ w_out[l]
        h = rms_norm(x, norm2_g[l])
        x = x + jnp.square(jax.nn.relu(h @ w_ff_in[l])) @ w_ff_out[l]
    return rms_norm(x, final_g)
```

```python
import functools
import math

import jax
import jax.numpy as jnp
import numpy as np
from jax import lax
from jax.experimental import pallas as pl
from jax.experimental.pallas import tpu as pltpu

F32 = jnp.float32
BF16 = jnp.bfloat16

NORM_EPS = 1e-5
V7X_VMEM_BYTES = 64 * 1024 * 1024
VMEM_LIMIT = 56 * 1024 * 1024
LANES = 128

SB_HEAD_DIM = 128
HG_HEAD_DIM = 128
HG_MIN_F = 1e-30
RW_HEAD_DIM = 64
RW_GN_EPS = 64e-5
RW_DECAY_RANK = 96
RW_AAA_RANK = 96
RW_GATE_RANK = 64
RW_LORA_COLS = RW_DECAY_RANK + RW_AAA_RANK + RW_GATE_RANK


def _cparams(*sem):
    return pltpu.CompilerParams(dimension_semantics=sem, vmem_limit_bytes=VMEM_LIMIT)


def _dot(a, b):
    return jnp.dot(a, b, preferred_element_type=F32)


def _dot_nt(a, b):
    return lax.dot_general(a, b, (((1,), (1,)), ((), ())), preferred_element_type=F32)


def _dot_tn(a, b):
    return lax.dot_general(a, b, (((0,), (0,)), ((), ())), preferred_element_type=F32)


def _split2(x):
    hi = x.astype(BF16)
    lo = (x - hi.astype(F32)).astype(BF16)
    return hi, lo


def _split3(x):
    hi = x.astype(BF16)
    r1 = x - hi.astype(F32)
    mid = r1.astype(BF16)
    lo = (r1 - mid.astype(F32)).astype(BF16)
    return hi, mid, lo


def _dot_exact_rhs01(a01, x, parts=3):
    ps = _split3(x) if parts == 3 else _split2(x)
    out = _dot(a01, ps[0])
    for p in ps[1:]:
        out = out + _dot(a01, p)
    return out


def _dot_exact_lhs(x, b01, parts=2):
    ps = _split3(x) if parts == 3 else _split2(x)
    out = _dot(ps[0], b01)
    for p in ps[1:]:
        out = out + _dot(p, b01)
    return out


def _rmsnorm_kernel(x_ref, g_ref, o_ref):
    x = x_ref[...]
    ms = jnp.mean(x * x, axis=-1, keepdims=True)
    o_ref[...] = (x * lax.rsqrt(ms + NORM_EPS) * g_ref[...]).astype(o_ref.dtype)


def rmsnorm(x, g, out_dtype, bm=512):
    m, d = x.shape
    bm = min(bm, m)
    return pl.pallas_call(
        _rmsnorm_kernel,
        out_shape=jax.ShapeDtypeStruct((m, d), out_dtype),
        grid=(m // bm,),
        in_specs=[pl.BlockSpec((bm, d), lambda i: (i, 0)),
                  pl.BlockSpec((1, d), lambda i: (0, 0))],
        out_specs=pl.BlockSpec((bm, d), lambda i: (i, 0)),
        compiler_params=_cparams("parallel"),
        name="rmsnorm",
    )(x, g.reshape(1, d).astype(F32))


def _matmul_kernel(*refs, nk, relu2, has_res):
    if has_res:
        a_ref, b_ref, r_ref, o_ref = refs[:4]
        rest = refs[4:]
    else:
        a_ref, b_ref, o_ref = refs[:3]
        r_ref = None
        rest = refs[3:]

    def finish(acc):
        if relu2:
            acc = jnp.square(jnp.maximum(acc, 0.0))
        if has_res:
            acc = acc + r_ref[...]
        o_ref[...] = acc.astype(o_ref.dtype)

    if nk == 1:
        finish(_dot(a_ref[...], b_ref[...]))
        return

    acc_ref = rest[0]
    k = pl.program_id(2)

    @pl.when(k == 0)
    def _():
        acc_ref[...] = _dot(a_ref[...], b_ref[...])

    @pl.when(k > 0)
    def _():
        acc_ref[...] += _dot(a_ref[...], b_ref[...])

    @pl.when(k == nk - 1)
    def _():
        finish(acc_ref[...])


def matmul(a, b, *, bm, bn, bk=None, relu2=False, residual=None, out_dtype=F32, name="matmul"):
    m, kdim = a.shape
    _, n = b.shape
    bm, bn = min(bm, m), min(bn, n)
    bk = kdim if bk is None else min(bk, kdim)
    assert m % bm == 0 and n % bn == 0 and kdim % bk == 0
    nk = kdim // bk
    has_res = residual is not None
    in_specs = [pl.BlockSpec((bm, bk), lambda i, j, k: (i, k)),
                pl.BlockSpec((bk, bn), lambda i, j, k: (k, j))]
    args = [a, b]
    if has_res:
        in_specs.append(pl.BlockSpec((bm, bn), lambda i, j, k: (i, j)))
        args.append(residual)
    return pl.pallas_call(
        functools.partial(_matmul_kernel, nk=nk, relu2=relu2, has_res=has_res),
        out_shape=jax.ShapeDtypeStruct((m, n), out_dtype),
        grid=(m // bm, n // bn, nk),
        in_specs=in_specs,
        out_specs=pl.BlockSpec((bm, bn), lambda i, j, k: (i, j)),
        scratch_shapes=[pltpu.VMEM((bm, bn), F32)] if nk > 1 else [],
        compiler_params=_cparams("parallel", "parallel", "arbitrary"),
        name=name,
    )(*args)


def _sb_kernel(q_ref, k_ref, v_ref, g_ref, o_ref, *, blk):
    i = pl.program_id(2)
    scale = SB_HEAD_DIM ** -0.5
    q = (q_ref[0] * scale).astype(BF16)
    row = lax.broadcasted_iota(jnp.int32, (blk, blk), 0)
    col = lax.broadcasted_iota(jnp.int32, (blk, blk), 1)
    before = col < row
    r2 = lax.broadcasted_iota(jnp.int32, (blk, 2 * blk), 0)
    c2 = lax.broadcasted_iota(jnp.int32, (blk, 2 * blk), 1)
    later = jnp.where((r2 > c2) | (c2 >= blk), 1.0, 0.0).astype(BF16)

    def block(j, carry, masked):
        acc, c = carry
        start = pl.multiple_of(j * blk, blk)
        kb = k_ref[0, pl.ds(start, blk), :].astype(BF16)
        vb = v_ref[0, pl.ds(start, blk), :].astype(BF16)
        z = _dot_nt(q, kb)
        log_beta = jnp.minimum(z, 0.0) - jnp.log1p(jnp.exp(-jnp.abs(z)))
        log_keep = log_beta - z
        if masked:
            log_keep = jnp.where(before, log_keep, 0.0)
        cs = _dot_exact_lhs(log_keep, later, parts=2)
        w = jnp.exp(log_beta + c + cs[:, :blk])
        if masked:
            w = jnp.where(before, w, 0.0)
        acc = acc + _dot(w.astype(BF16), vb)
        return acc, c + cs[:, blk:]

    carry = (jnp.zeros((blk, SB_HEAD_DIM), F32), jnp.zeros((blk, blk), F32))
    carry = block(i, carry, True)
    acc, _ = lax.fori_loop(0, i, lambda t, cr: block(i - 1 - t, cr, False), carry)
    ms = jnp.mean(acc * acc, axis=-1, keepdims=True)
    o_ref[0] = (acc * lax.rsqrt(ms + NORM_EPS) * g_ref[...]).astype(o_ref.dtype)


def sb_attention(proj_sb, norm_g, *, blk=128):
    b, t, w3 = proj_sb.shape
    w = w3 // 3
    nh = w // SB_HEAD_DIM
    blk = min(blk, t)
    return pl.pallas_call(
        functools.partial(_sb_kernel, blk=blk),
        out_shape=jax.ShapeDtypeStruct((b, t, w), BF16),
        grid=(b, nh, t // blk),
        in_specs=[pl.BlockSpec((1, blk, SB_HEAD_DIM), lambda bi, h, i: (bi, i, h)),
                  pl.BlockSpec((1, t, SB_HEAD_DIM), lambda bi, h, i: (bi, 0, nh + h)),
                  pl.BlockSpec((1, t, SB_HEAD_DIM), lambda bi, h, i: (bi, 0, 2 * nh + h)),
                  pl.BlockSpec((1, SB_HEAD_DIM), lambda bi, h, i: (0, h))],
        out_specs=pl.BlockSpec((1, blk, SB_HEAD_DIM), lambda bi, h, i: (bi, i, h)),
        compiler_params=_cparams("parallel", "parallel", "arbitrary"),
        name="sb_attention",
    )(proj_sb, proj_sb, proj_sb, norm_g.reshape(1, w).astype(F32))


def _hg_level_halves(chunk):
    halves = []
    m = chunk // 2
    while m >= 1:
        halves.append(m)
        m //= 2
    return halves


def _hg_decay_selectors(chunk):
    t = lax.broadcasted_iota(jnp.int32, (chunk, chunk), 0)
    j = lax.broadcasted_iota(jnp.int32, (chunk, chunk), 1)
    mats = [j <= t, j > t]
    for m in _hg_level_halves(chunk):
        mid = (t // (2 * m)) * (2 * m) + m - 1
        upper = (t % (2 * m)) >= m
        mats.append((upper & (j > mid) & (j <= t)) | (~upper & (j > t) & (j <= mid)))
    return jnp.concatenate([jnp.where(mm, 1.0, 0.0).astype(BF16) for mm in mats], axis=0)


def _hg_kernel(q_ref, f_ref, i_ref, g_ref, lbp_ref, ng_ref, o_ref, *, chunk, layer, seq):
    halves = _hg_level_halves(chunk)
    sel = _hg_decay_selectors(chunk)

    p = lbp_ref[...]
    e = jnp.exp(p - jnp.max(p, axis=0, keepdims=True))
    probs = e / jnp.sum(e, axis=0, keepdims=True)
    lb = jnp.zeros((1, HG_HEAD_DIM), F32)
    for l in range(1, layer + 1):
        lb = lb + probs[l:l + 1, :]

    trow = lax.broadcasted_iota(jnp.int32, (chunk, HG_HEAD_DIM), 0)
    ts = lax.broadcasted_iota(jnp.int32, (chunk, chunk), 0)
    ss = lax.broadcasted_iota(jnp.int32, (chunk, chunk), 1)
    ng = ng_ref[...]

    def step(c, state_t):
        start = pl.multiple_of(c * chunk, chunk)
        rows = pl.ds(start, chunk)
        q = q_ref[0, rows, :]
        fr = f_ref[0, rows, :]
        v = i_ref[0, rows, :]
        gate = g_ref[0, rows, :]
        f = lb + (1.0 - lb) * jax.nn.sigmoid(fr)
        log_f = jnp.log(jnp.maximum(f, HG_MIN_F))
        key = (1.0 - lb) * jax.nn.sigmoid(-fr)
        ex = jnp.exp(_dot_exact_rhs01(sel, log_f, parts=3))
        e_pre = ex[0:chunk]
        e_suf = ex[chunk:2 * chunk]
        vb = v.astype(BF16)

        scores = jnp.zeros((chunk, chunk), F32)
        for li, m in enumerate(halves):
            el = ex[(2 + li) * chunk:(3 + li) * chunk]
            upper = (trow % (2 * m)) >= m
            ql = jnp.where(upper, q * el, 0.0).astype(BF16)
            kl = jnp.where(upper, 0.0, key * el).astype(BF16)
            sl = _dot_nt(ql, kl)
            if 2 * m < chunk:
                sl = jnp.where((ts // (2 * m)) == (ss // (2 * m)), sl, 0.0)
            scores = scores + sl
        diag = jnp.sum(q * key, axis=-1, keepdims=True)
        o = (_dot(scores.astype(BF16), vb) + diag * v
             + _dot_nt((q * e_pre).astype(BF16), state_t.astype(BF16)))
        ms = jnp.mean(o * o, axis=-1, keepdims=True)
        o = o * lax.rsqrt(ms + NORM_EPS) * ng
        o_ref[0, rows, :] = (o * (gate * jax.nn.sigmoid(gate))).astype(o_ref.dtype)

        decay_all = ex[chunk - 1:chunk, :]
        return state_t * decay_all + _dot_tn(vb, (key * e_suf).astype(BF16))

    lax.fori_loop(0, seq // chunk, step, jnp.zeros((HG_HEAD_DIM, HG_HEAD_DIM), F32))


def hgrn2(proj_hg, lb_param, norm_g, *, layer, chunk=128):
    b, t, w4 = proj_hg.shape
    w = w4 // 4
    nh = w // HG_HEAD_DIM
    chunk = min(chunk, t)
    depth = lb_param.shape[0]

    def col(g):
        return pl.BlockSpec((1, t, HG_HEAD_DIM), lambda bi, h: (bi, 0, g * nh + h))

    return pl.pallas_call(
        functools.partial(_hg_kernel, chunk=chunk, layer=layer, seq=t),
        out_shape=jax.ShapeDtypeStruct((b, t, w), BF16),
        grid=(b, nh),
        in_specs=[col(0), col(1), col(2), col(3),
                  pl.BlockSpec((depth, HG_HEAD_DIM), lambda bi, h: (0, h)),
                  pl.BlockSpec((1, HG_HEAD_DIM), lambda bi, h: (0, 0))],
        out_specs=pl.BlockSpec((1, t, HG_HEAD_DIM), lambda bi, h: (bi, 0, h)),
        compiler_params=_cparams("parallel", "parallel"),
        name="hgrn2",
    )(proj_hg, proj_hg, proj_hg, proj_hg, lb_param.astype(F32),
      norm_g.reshape(1, HG_HEAD_DIM).astype(F32))


RW_GROUP_HEADS = 4
RW_GROUP = RW_GROUP_HEADS * RW_HEAD_DIM


def _head_ones(n, head):
    r = lax.broadcasted_iota(jnp.int32, (n, n), 0)
    c = lax.broadcasted_iota(jnp.int32, (n, n), 1)
    return jnp.where((r // head) == (c // head), 1.0, 0.0).astype(BF16)


def _head_sums(x, ones_bd):
    n = ones_bd.shape[0]
    outs = [_dot_exact_lhs(x[:, s:s + n], ones_bd, parts=2) for s in range(0, x.shape[1], n)]
    return outs[0] if len(outs) == 1 else jnp.concatenate(outs, axis=1)


def _softplus(y):
    return jnp.maximum(y, 0.0) + jnp.log1p(jnp.exp(-jnp.abs(y)))


def _rw_prep_kernel(x_ref, xp_ref, mu_ref, w0_ref, a0_ref, kk_ref, ka_ref, rk_ref,
                    w2w_ref, w2a_ref, w2g_ref,
                    r_out, lw_out, k_out, v_out, kk_out, kb_out, g_out, bonus_out, *, width):
    i = pl.program_id(1)
    x = x_ref[0]
    bt = x.shape[0]
    prev_row = jnp.where(i == 0, 0.0, xp_ref[0, 7:8, :])
    row = lax.broadcasted_iota(jnp.int32, x.shape, 0)
    prev = jnp.where(row == 0, prev_row, pltpu.roll(x, 1, axis=0))
    xs = x + mu_ref[...] * (prev - x)

    lora = xs[:, 3 * width:]
    w_log = -_softplus(-(w0_ref[...] + _dot(jnp.tanh(lora).astype(BF16), w2w_ref[...]))) - 0.5
    log_decay = -jnp.exp(w_log)
    a = jax.nn.sigmoid(a0_ref[...] + _dot(lora.astype(BF16), w2a_ref[...]))
    g = _dot(jax.nn.sigmoid(lora).astype(BF16), w2g_ref[...])

    r = xs[:, :width]
    k = xs[:, width:2 * width]
    v = xs[:, 2 * width:3 * width]
    ones_bd = _head_ones(RW_GROUP, RW_HEAD_DIM)
    kk = k * kk_ref[...]
    kk = kk * lax.rsqrt(jnp.maximum(_head_sums(kk * kk, ones_bd), 1e-24))
    k = k * (1.0 + (a - 1.0) * ka_ref[...])
    bonus = _head_sums(r * k * rk_ref[...], ones_bd) * v

    r_out[0] = r
    lw_out[0] = log_decay
    k_out[0] = k
    v_out[0] = v
    kk_out[0] = kk
    kb_out[0] = kk * a
    g_out[0] = g
    bonus_out[0] = bonus


def _rw_prep(rw, mu, w0, w_w2, a0, w_a2, w_g2, k_k, k_a, r_k, *, bt=128):
    b, t, cols = rw.shape
    width = (cols - RW_LORA_COLS) // 3
    bt = min(bt, t)

    def padded(w2, first_row):
        full = jnp.zeros((RW_LORA_COLS, width), F32)
        return lax.dynamic_update_slice(full, w2.astype(F32), (first_row, 0)).astype(BF16)

    w2w = padded(w_w2, 0)
    w2a = padded(w_a2, RW_DECAY_RANK)
    w2g = padded(w_g2, RW_DECAY_RANK + RW_AAA_RANK)

    def vec(p):
        return p.reshape(1, -1).astype(F32)

    def vspec(n):
        return pl.BlockSpec((1, n), lambda bi, i: (0, 0))

    wspec = pl.BlockSpec((RW_LORA_COLS, width), lambda bi, i: (0, 0))
    ospec = pl.BlockSpec((1, bt, width), lambda bi, i: (bi, i, 0))
    return pl.pallas_call(
        functools.partial(_rw_prep_kernel, width=width),
        out_shape=[jax.ShapeDtypeStruct((b, t, width), F32)] * 8,
        grid=(b, t // bt),
        in_specs=[pl.BlockSpec((1, bt, cols), lambda bi, i: (bi, i, 0)),
                  pl.BlockSpec((1, 8, cols), lambda bi, i: (bi, jnp.maximum(i * (bt // 8) - 1, 0), 0)),
                  vspec(cols), vspec(width), vspec(width), vspec(width), vspec(width), vspec(width),
                  wspec, wspec, wspec],
        out_specs=[ospec] * 8,
        compiler_params=_cparams("parallel", "arbitrary"),
        name="rwkv7_prep",
    )(rw, rw, vec(mu), vec(w0), vec(a0), vec(k_k), vec(k_a), vec(r_k), w2w, w2a, w2g)


def _tile_rows(x, n):
    return jnp.concatenate([x] * n, axis=0)


def _rw_rec_kernel(r_ref, lw_ref, k_ref, v_ref, kk_ref, kb_ref, g_ref, bonus_ref,
                   lnw_ref, lnb_ref, o_ref, ht_ref, *, chunk, groups, inv_parts):
    nh, hd, gw = RW_GROUP_HEADS, RW_HEAD_DIM, RW_GROUP
    wide = nh * chunk
    tb = r_ref.shape[1]

    @pl.when(pl.program_id(2) == 0)
    def _():
        ht_ref[...] = jnp.zeros_like(ht_ref)

    def iota(shape, d):
        return lax.broadcasted_iota(jnp.int32, shape, d)

    ltri = jnp.where(iota((chunk, chunk), 1) <= iota((chunk, chunk), 0), 1.0, 0.0).astype(BF16)
    row_cw = iota((chunk, wide), 0)
    pos_cw = iota((chunk, wide), 1) % chunk
    strict = pos_cw < row_cw
    incl = pos_cw <= row_cw
    eye_t = jnp.where(pos_cw == row_cw, 1.0, 0.0)
    off_diag = []
    s = 1
    while s < chunk:
        off_diag.append(((row_cw // (2 * s)) == (pos_cw // (2 * s)))
                        & ((row_cw % (2 * s)) >= s) & ((pos_cw % (2 * s)) < s))
        s *= 2
    bd_wk = (iota((wide, gw), 0) // chunk) == (iota((wide, gw), 1) // hd)
    bd_ww = (iota((wide, wide), 0) // chunk) == (iota((wide, wide), 1) // chunk)
    head_mask = (iota((gw, gw), 0) // hd) == (iota((gw, gw), 1) // hd)
    ones_bd = _head_ones(gw, hd)

    def expand_k(x):
        return jnp.where(bd_wk, _tile_rows(x, nh), 0.0).astype(BF16)

    def expand_w(x):
        return jnp.where(bd_ww, _tile_rows(x, nh), 0.0)

    def mm(a, b):
        if inv_parts == 1:
            return _dot(a.astype(BF16), b.astype(BF16))
        ah, al = _split2(a)
        bh, bl = _split2(b)
        return _dot(ah, bh) + (_dot(ah, bl) + _dot(al, bh))

    def step(c, _):
        rows = pl.ds(pl.multiple_of(c * chunk, chunk), chunk)
        for gi in range(groups):
            lanes = slice(gi * gw, (gi + 1) * gw)
            r = r_ref[0, rows, lanes]
            lw = lw_ref[0, rows, lanes]
            k = k_ref[0, rows, lanes]
            v = v_ref[0, rows, lanes]
            kk = kk_ref[0, rows, lanes]
            kb = kb_ref[0, rows, lanes]

            gam = _dot_exact_rhs01(ltri, lw, parts=3)
            gam_end = gam[chunk - 1:chunk, :]
            e_neg = jnp.exp(-gam)
            e_suf = jnp.exp(gam_end - gam)
            a_t = -kk * jnp.exp(gam - lw)
            r_t = r * jnp.exp(gam)
            lhs = jnp.concatenate([a_t, r_t], axis=0).astype(BF16)

            sb = _dot_nt(lhs, expand_k(kb * e_neg))
            sk = _dot_nt(lhs, expand_k(k * e_neg))
            a_ab = jnp.where(strict, sb[:chunk], 0.0)
            a_rb = jnp.where(incl, sb[chunk:], 0.0)
            a_ak = jnp.where(strict, sk[:chunk], 0.0)
            a_rk = jnp.where(incl, sk[chunk:], 0.0)

            p_acc = eye_t + jnp.where(off_diag[0], a_ab, 0.0)
            for li in range(1, len(off_diag)):
                n_off = jnp.where(off_diag[li], a_ab, 0.0)
                p_acc = p_acc + mm(mm(p_acc, expand_w(n_off)), expand_w(p_acc))

            ht = ht_ref[gi]
            from_state = _dot_nt(lhs, ht.astype(BF16))
            from_v = _dot(jnp.concatenate([a_ak, a_rk], axis=0).astype(BF16), expand_k(v))
            u = _dot(p_acc.astype(BF16), expand_k(from_state[:chunk] + from_v[:chunk]))
            y = from_state[chunk:] + from_v[chunk:] + _dot(a_rb.astype(BF16), expand_k(u))

            upd = _dot_tn(jnp.concatenate([u, v], axis=0).astype(BF16),
                          jnp.concatenate([kb * e_suf, k * e_suf], axis=0).astype(BF16))
            ht_ref[gi] = ht * jnp.exp(gam_end) + jnp.where(head_mask, upd, 0.0)

            mean = _head_sums(y, ones_bd) * (1.0 / hd)
            yc = y - mean
            var = _head_sums(yc * yc, ones_bd) * (1.0 / hd)
            yn = yc * lax.rsqrt(var + RW_GN_EPS) * lnw_ref[:, lanes] + lnb_ref[:, lanes]
            out = (yn + bonus_ref[0, rows, lanes]) * g_ref[0, rows, lanes]
            o_ref[0, rows, lanes] = out.astype(o_ref.dtype)
        return 0

    lax.fori_loop(0, tb // chunk, step, 0)


def _rw_recurrence(r, lw, k, v, kk, kb, g, bonus, lnx_w, lnx_b, *, chunk=64, groups=1, tb=512,
                   inv_parts=3):
    b, t, width = r.shape
    tb = min(tb, t)
    bw = groups * RW_GROUP
    spec = pl.BlockSpec((1, tb, bw), lambda bi, gi, ti: (bi, ti, gi))
    vspec = pl.BlockSpec((1, bw), lambda bi, gi, ti: (0, gi))
    return pl.pallas_call(
        functools.partial(_rw_rec_kernel, chunk=chunk, groups=groups, inv_parts=inv_parts),
        out_shape=jax.ShapeDtypeStruct((b, t, width), BF16),
        grid=(b, width // bw, t // tb),
        in_specs=[spec] * 8 + [vspec, vspec],
        out_specs=spec,
        scratch_shapes=[pltpu.VMEM((groups, RW_GROUP, RW_GROUP), F32)],
        compiler_params=_cparams("parallel", "parallel", "arbitrary"),
        name="rwkv7_recurrence",
    )(r, lw, k, v, kk, kb, g, bonus,
      lnx_w.reshape(1, width).astype(F32), lnx_b.reshape(1, width).astype(F32))


def rwkv7(rw, mu, w0, w_w2, a0, w_a2, w_g2, k_k, k_a, r_k, lnx_w, lnx_b, **rec_kw):
    parts = _rw_prep(rw, mu, w0, w_w2, a0, w_a2, w_g2, k_k, k_a, r_k)
    return _rw_recurrence(*parts, lnx_w, lnx_b, **rec_kw)


def kernel(x, norm1_g, w_in, sb_norm_g, hg_lb_param, hg_norm_g, rw_mu, rw_w0, rw_w_w2, rw_a0,
           rw_w_a2, rw_w_g2, rw_k_k, rw_k_a, rw_r_k, rw_lnx_w, rw_lnx_b, w_out, norm2_g,
           w_ff_in, w_ff_out, final_g):
    b, t, d = x.shape
    depth = norm1_g.shape[0]
    sb_w = sb_norm_g.shape[1]
    hg_w = hg_lb_param.shape[1]
    sb_cols, hg_cols = 3 * sb_w, 4 * hg_w
    xf = x.reshape(b * t, d)
    for l in range(depth):
        h = rmsnorm(xf, norm1_g[l], BF16)
        w = w_in[l]
        w_sb = w[:, :sb_cols].astype(BF16)
        w_hg = w[:, sb_cols:sb_cols + hg_cols].astype(BF16)
        w_rw = w[:, sb_cols + hg_cols:].astype(BF16)
        p_sb = matmul(h, w_sb, bm=1024, bn=1024, name="proj_sb").reshape(b, t, -1)
        p_hg = matmul(h, w_hg, bm=1024, bn=1024, name="proj_hg").reshape(b, t, -1)
        p_rw = matmul(h, w_rw, bm=1024, bn=1280, name="proj_rw").reshape(b, t, -1)
        o_sb = sb_attention(p_sb, sb_norm_g[l])
        o_hg = hgrn2(p_hg, hg_lb_param, hg_norm_g[l], layer=l)
        o_rw = rwkv7(p_rw, rw_mu[l], rw_w0[l], rw_w_w2[l], rw_a0[l], rw_w_a2[l], rw_w_g2[l],
                     rw_k_k[l], rw_k_a[l], rw_r_k[l], rw_lnx_w[l], rw_lnx_b[l])
        mix = jnp.concatenate([o_sb, o_hg, o_rw], axis=-1).reshape(b * t, d)
        xf = matmul(mix, w_out[l].astype(BF16), bm=1024, bn=1024, residual=xf, name="out_proj")
        h = rmsnorm(xf, norm2_g[l], BF16)
        ff = matmul(h, w_ff_in[l].astype(BF16), bm=1024, bn=1024, relu2=True, out_dtype=BF16,
                    name="ff_in")
        xf = matmul(ff, w_ff_out[l].astype(BF16), bm=1024, bn=1024, bk=2048, residual=xf,
                    name="ff_out")
    return rmsnorm(xf, final_g, F32).reshape(b, t, d)
```

```python
import functools
import math

import jax
import jax.numpy as jnp
import numpy as np
from jax import lax
from jax.experimental import pallas as pl
from jax.experimental.pallas import tpu as pltpu

F32 = jnp.float32
BF16 = jnp.bfloat16

NORM_EPS = 1e-5
V7X_VMEM_BYTES = 64 * 1024 * 1024
VMEM_LIMIT = 56 * 1024 * 1024
LANES = 128

SB_HEAD_DIM = 128
SB_LOG_WEIGHT_CUTOFF = -90.0
HG_HEAD_DIM = 128
HG_MIN_F = 1e-30
RW_HEAD_DIM = 64
RW_GN_EPS = 64e-5
RW_DECAY_RANK = 96
RW_AAA_RANK = 96
RW_GATE_RANK = 64
RW_LORA_COLS = RW_DECAY_RANK + RW_AAA_RANK + RW_GATE_RANK


def _cparams(*sem):
    return pltpu.CompilerParams(dimension_semantics=sem, vmem_limit_bytes=VMEM_LIMIT)


def _dot(a, b):
    return jnp.dot(a, b, preferred_element_type=F32)


def _dot_nt(a, b):
    return lax.dot_general(a, b, (((1,), (1,)), ((), ())), preferred_element_type=F32)


def _dot_tn(a, b):
    return lax.dot_general(a, b, (((0,), (0,)), ((), ())), preferred_element_type=F32)


def _split2(x):
    hi = x.astype(BF16)
    lo = (x - hi.astype(F32)).astype(BF16)
    return hi, lo


def _split3(x):
    hi = x.astype(BF16)
    r1 = x - hi.astype(F32)
    mid = r1.astype(BF16)
    lo = (r1 - mid.astype(F32)).astype(BF16)
    return hi, mid, lo


def _dot_exact_rhs01(a01, x, parts=3):
    ps = _split3(x) if parts == 3 else _split2(x)
    out = _dot(a01, ps[0])
    for p in ps[1:]:
        out = out + _dot(a01, p)
    return out


def _dot_exact_lhs(x, b01, parts=2):
    ps = _split3(x) if parts == 3 else _split2(x)
    out = _dot(ps[0], b01)
    for p in ps[1:]:
        out = out + _dot(p, b01)
    return out


def _rmsnorm_kernel(x_ref, g_ref, o_ref):
    x = x_ref[...]
    ms = jnp.mean(x * x, axis=-1, keepdims=True)
    o_ref[...] = (x * lax.rsqrt(ms + NORM_EPS) * g_ref[...]).astype(o_ref.dtype)


def rmsnorm(x, g, out_dtype, bm=512):
    m, d = x.shape
    bm = min(bm, m)
    return pl.pallas_call(
        _rmsnorm_kernel,
        out_shape=jax.ShapeDtypeStruct((m, d), out_dtype),
        grid=(m // bm,),
        in_specs=[pl.BlockSpec((bm, d), lambda i: (i, 0)),
                  pl.BlockSpec((1, d), lambda i: (0, 0))],
        out_specs=pl.BlockSpec((bm, d), lambda i: (i, 0)),
        compiler_params=_cparams("parallel"),
        name="rmsnorm",
    )(x, g.reshape(1, d).astype(F32))


def _matmul_kernel(*refs, nk, relu2, has_res):
    if has_res:
        a_ref, b_ref, r_ref, o_ref = refs[:4]
        rest = refs[4:]
    else:
        a_ref, b_ref, o_ref = refs[:3]
        r_ref = None
        rest = refs[3:]

    def finish(acc):
        if relu2:
            acc = jnp.square(jnp.maximum(acc, 0.0))
        if has_res:
            acc = acc + r_ref[...]
        o_ref[...] = acc.astype(o_ref.dtype)

    if nk == 1:
        finish(_dot(a_ref[...], b_ref[...]))
        return

    acc_ref = rest[0]
    k = pl.program_id(2)

    @pl.when(k == 0)
    def _():
        acc_ref[...] = _dot(a_ref[...], b_ref[...])

    @pl.when(k > 0)
    def _():
        acc_ref[...] += _dot(a_ref[...], b_ref[...])

    @pl.when(k == nk - 1)
    def _():
        finish(acc_ref[...])


def matmul(a, b, *, bm, bn, bk=None, relu2=False, residual=None, out_dtype=F32, name="matmul"):
    m, kdim = a.shape
    _, n = b.shape
    bm, bn = min(bm, m), min(bn, n)
    bk = kdim if bk is None else min(bk, kdim)
    assert m % bm == 0 and n % bn == 0 and kdim % bk == 0
    nk = kdim // bk
    has_res = residual is not None
    in_specs = [pl.BlockSpec((bm, bk), lambda i, j, k: (i, k)),
                pl.BlockSpec((bk, bn), lambda i, j, k: (k, j))]
    args = [a, b]
    if has_res:
        in_specs.append(pl.BlockSpec((bm, bn), lambda i, j, k: (i, j)))
        args.append(residual)
    return pl.pallas_call(
        functools.partial(_matmul_kernel, nk=nk, relu2=relu2, has_res=has_res),
        out_shape=jax.ShapeDtypeStruct((m, n), out_dtype),
        grid=(m // bm, n // bn, nk),
        in_specs=in_specs,
        out_specs=pl.BlockSpec((bm, bn), lambda i, j, k: (i, j)),
        scratch_shapes=[pltpu.VMEM((bm, bn), F32)] if nk > 1 else [],
        compiler_params=_cparams("parallel", "parallel", "arbitrary"),
        name=name,
    )(*args)


def _sb_kernel(q_ref, k_ref, v_ref, g_ref, o_ref, *, blk, heads):
    i = pl.program_id(2)
    scale = SB_HEAD_DIM ** -0.5
    hd = SB_HEAD_DIM
    qs = [(q_ref[0, :, h * hd:(h + 1) * hd] * scale).astype(BF16) for h in range(heads)]
    row = lax.broadcasted_iota(jnp.int32, (blk, blk), 0)
    col = lax.broadcasted_iota(jnp.int32, (blk, blk), 1)
    before = col < row
    r2 = lax.broadcasted_iota(jnp.int32, (blk, 2 * blk), 0)
    c2 = lax.broadcasted_iota(jnp.int32, (blk, 2 * blk), 1)
    later = jnp.where((r2 > c2) | (c2 >= blk), 1.0, 0.0).astype(BF16)

    def block(j, carry, masked):
        start = pl.multiple_of(j * blk, blk)
        hs = range(heads)
        kbs = [k_ref[0, pl.ds(start, blk), h * hd:(h + 1) * hd].astype(BF16) for h in hs]
        vbs = [v_ref[0, pl.ds(start, blk), h * hd:(h + 1) * hd].astype(BF16) for h in hs]
        zs = [_dot_nt(qs[h], kbs[h]) for h in hs]
        log_betas = [jnp.minimum(z, 0.0) - jnp.log1p(jnp.exp(-jnp.abs(z))) for z in zs]
        log_keeps = [lb - z for lb, z in zip(log_betas, zs)]
        if masked:
            log_keeps = [jnp.where(before, lk, 0.0) for lk in log_keeps]
        css = [_dot_exact_lhs(lk, later, parts=2) for lk in log_keeps]
        ws = [jnp.exp(log_betas[h] + carry[2 * h + 1] + css[h][:, :blk]) for h in hs]
        if masked:
            ws = [jnp.where(before, w, 0.0) for w in ws]
        out = []
        for h in hs:
            out += [carry[2 * h] + _dot(ws[h].astype(BF16), vbs[h]), carry[2 * h + 1] + css[h][:, blk:]]
        return tuple(out)

    carry = (jnp.zeros((blk, hd), F32), jnp.zeros((blk, blk), F32)) * heads
    carry = block(i, carry, True)

    def alive(state):
        j = state[0]
        top = state[2]
        for h in range(1, heads):
            top = jnp.maximum(top, state[2 + 2 * h])
        return (j >= 0) & (jnp.max(top) > SB_LOG_WEIGHT_CUTOFF)

    def body(state):
        j = state[0]
        return (j - 1,) + block(j, state[1:], False)

    state = lax.while_loop(alive, body, (i - 1,) + carry)
    for h in range(heads):
        acc = state[1 + 2 * h]
        ms = jnp.mean(acc * acc, axis=-1, keepdims=True)
        o_ref[0, :, h * hd:(h + 1) * hd] = (
            acc * lax.rsqrt(ms + NORM_EPS) * g_ref[:, h * hd:(h + 1) * hd]).astype(o_ref.dtype)


def sb_attention(proj_sb, norm_g, *, blk=128, heads=4):
    b, t, w3 = proj_sb.shape
    w = w3 // 3
    bw = heads * SB_HEAD_DIM
    nh = w // bw
    blk = min(blk, t)
    return pl.pallas_call(
        functools.partial(_sb_kernel, blk=blk, heads=heads),
        out_shape=jax.ShapeDtypeStruct((b, t, w), BF16),
        grid=(b, nh, t // blk),
        in_specs=[pl.BlockSpec((1, blk, bw), lambda bi, h, i: (bi, i, h)),
                  pl.BlockSpec((1, t, bw), lambda bi, h, i: (bi, 0, nh + h)),
                  pl.BlockSpec((1, t, bw), lambda bi, h, i: (bi, 0, 2 * nh + h)),
                  pl.BlockSpec((1, bw), lambda bi, h, i: (0, h))],
        out_specs=pl.BlockSpec((1, blk, bw), lambda bi, h, i: (bi, i, h)),
        compiler_params=_cparams("parallel", "parallel", "arbitrary"),
        name="sb_attention",
    )(proj_sb, proj_sb, proj_sb, norm_g.reshape(1, w).astype(F32))


def _hg_level_halves(chunk):
    halves = []
    m = chunk // 2
    while m >= 1:
        halves.append(m)
        m //= 2
    return halves


def _hg_decay_selectors(chunk):
    t = lax.broadcasted_iota(jnp.int32, (chunk, chunk), 0)
    j = lax.broadcasted_iota(jnp.int32, (chunk, chunk), 1)
    mats = [j <= t, j > t]
    for m in _hg_level_halves(chunk):
        mid = (t // (2 * m)) * (2 * m) + m - 1
        upper = (t % (2 * m)) >= m
        mats.append((upper & (j > mid) & (j <= t)) | (~upper & (j > t) & (j <= mid)))
    return jnp.concatenate([jnp.where(mm, 1.0, 0.0).astype(BF16) for mm in mats], axis=0)


def _hg_kernel(q_ref, f_ref, i_ref, g_ref, lbp_ref, ng_ref, o_ref, st_ref, *, chunk, layer, heads):
    hd = HG_HEAD_DIM
    tb = q_ref.shape[1]

    @pl.when(pl.program_id(2) == 0)
    def _():
        st_ref[...] = jnp.zeros_like(st_ref)

    halves = _hg_level_halves(chunk)
    sel = _hg_decay_selectors(chunk)

    p = lbp_ref[...]
    e = jnp.exp(p - jnp.max(p, axis=0, keepdims=True))
    probs = e / jnp.sum(e, axis=0, keepdims=True)
    lb = jnp.zeros((1, heads * hd), F32)
    for l in range(1, layer + 1):
        lb = lb + probs[l:l + 1, :]

    trow = lax.broadcasted_iota(jnp.int32, (chunk, heads * hd), 0)
    ts = lax.broadcasted_iota(jnp.int32, (chunk, chunk), 0)
    ss = lax.broadcasted_iota(jnp.int32, (chunk, chunk), 1)
    ng = ng_ref[...]

    def step(c, _):
        start = pl.multiple_of(c * chunk, chunk)
        rows = pl.ds(start, chunk)
        q2 = q_ref[0, rows, :]
        fr = f_ref[0, rows, :]
        v2 = i_ref[0, rows, :]
        gate = g_ref[0, rows, :]
        f = lb + (1.0 - lb) * jax.nn.sigmoid(fr)
        log_f = jnp.log(jnp.maximum(f, HG_MIN_F))
        key2 = (1.0 - lb) * jax.nn.sigmoid(-fr)
        ex2 = jnp.exp(_dot_exact_rhs01(sel, log_f, parts=3))
        silu = gate * jax.nn.sigmoid(gate)

        hs = range(heads)
        lanes = [slice(h * hd, (h + 1) * hd) for h in hs]
        states = [st_ref[h] for h in hs]
        vbs = [v2[:, l].astype(BF16) for l in lanes]
        scores = [None] * heads
        for li, m in enumerate(halves):
            el = ex2[(2 + li) * chunk:(3 + li) * chunk]
            upper = (trow % (2 * m)) >= m
            ql = jnp.where(upper, q2 * el, 0.0).astype(BF16)
            kl = jnp.where(upper, 0.0, key2 * el).astype(BF16)
            for h in hs:
                sl = _dot_nt(ql[:, lanes[h]], kl[:, lanes[h]])
                if 2 * m < chunk:
                    sl = jnp.where((ts // (2 * m)) == (ss // (2 * m)), sl, 0.0)
                scores[h] = sl if scores[h] is None else scores[h] + sl
        q_pre = (q2 * ex2[0:chunk]).astype(BF16)
        k_suf = (key2 * ex2[chunk:2 * chunk]).astype(BF16)
        qk = q2 * key2
        inter = [_dot_nt(q_pre[:, lanes[h]], states[h].astype(BF16)) for h in hs]
        intra = [_dot(scores[h].astype(BF16), vbs[h]) for h in hs]
        upd = [_dot_tn(vbs[h], k_suf[:, lanes[h]]) for h in hs]
        decay_all = ex2[chunk - 1:chunk, :]
        for h in hs:
            l = lanes[h]
            st_ref[h] = states[h] * decay_all[:, l] + upd[h]
            diag = jnp.sum(qk[:, l], axis=-1, keepdims=True)
            o = intra[h] + diag * v2[:, l] + inter[h]
            ms = jnp.mean(o * o, axis=-1, keepdims=True)
            o = o * lax.rsqrt(ms + NORM_EPS) * ng
            o_ref[0, rows, l] = (o * silu[:, l]).astype(o_ref.dtype)
        return 0

    lax.fori_loop(0, tb // chunk, step, 0)


def hgrn2(proj_hg, lb_param, norm_g, *, layer, chunk=128, heads=4, tb=512):
    b, t, w4 = proj_hg.shape
    w = w4 // 4
    bw = heads * HG_HEAD_DIM
    nh = w // bw
    tb = min(tb, t)
    chunk = min(chunk, tb)
    depth = lb_param.shape[0]

    def col(g):
        return pl.BlockSpec((1, tb, bw), lambda bi, h, ti: (bi, ti, g * nh + h))

    return pl.pallas_call(
        functools.partial(_hg_kernel, chunk=chunk, layer=layer, heads=heads),
        out_shape=jax.ShapeDtypeStruct((b, t, w), BF16),
        grid=(b, nh, t // tb),
        in_specs=[col(0), col(1), col(2), col(3),
                  pl.BlockSpec((depth, bw), lambda bi, h, ti: (0, h)),
                  pl.BlockSpec((1, HG_HEAD_DIM), lambda bi, h, ti: (0, 0))],
        out_specs=pl.BlockSpec((1, tb, bw), lambda bi, h, ti: (bi, ti, h)),
        scratch_shapes=[pltpu.VMEM((heads, HG_HEAD_DIM, HG_HEAD_DIM), F32)],
        compiler_params=_cparams("parallel", "parallel", "arbitrary"),
        name="hgrn2",
    )(proj_hg, proj_hg, proj_hg, proj_hg, lb_param.astype(F32),
      norm_g.reshape(1, HG_HEAD_DIM).astype(F32))


RW_GROUP_HEADS = 4
RW_GROUP = RW_GROUP_HEADS * RW_HEAD_DIM


def _head_ones(n, head):
    r = lax.broadcasted_iota(jnp.int32, (n, n), 0)
    c = lax.broadcasted_iota(jnp.int32, (n, n), 1)
    return jnp.where((r // head) == (c // head), 1.0, 0.0).astype(BF16)


def _head_sums(x, ones_bd):
    n = ones_bd.shape[0]
    outs = [_dot_exact_lhs(x[:, s:s + n], ones_bd, parts=2) for s in range(0, x.shape[1], n)]
    return outs[0] if len(outs) == 1 else jnp.concatenate(outs, axis=1)


def _softplus(y):
    return jnp.maximum(y, 0.0) + jnp.log1p(jnp.exp(-jnp.abs(y)))


def _rw_prep_kernel(x_ref, xp_ref, mu_ref, w0_ref, a0_ref, kk_ref, ka_ref, rk_ref,
                    w2w_ref, w2a_ref, w2g_ref,
                    r_out, lw_out, k_out, v_out, kk_out, kb_out, g_out, bonus_out, *, width):
    i = pl.program_id(1)
    x = x_ref[0]
    bt = x.shape[0]
    prev_row = jnp.where(i == 0, 0.0, xp_ref[0, 7:8, :])
    row = lax.broadcasted_iota(jnp.int32, x.shape, 0)
    prev = jnp.where(row == 0, prev_row, pltpu.roll(x, 1, axis=0))
    xs = x + mu_ref[...] * (prev - x)

    lora = xs[:, 3 * width:]
    w_log = -_softplus(-(w0_ref[...] + _dot(jnp.tanh(lora).astype(BF16), w2w_ref[...]))) - 0.5
    log_decay = -jnp.exp(w_log)
    a = jax.nn.sigmoid(a0_ref[...] + _dot(lora.astype(BF16), w2a_ref[...]))
    g = _dot(jax.nn.sigmoid(lora).astype(BF16), w2g_ref[...])

    r = xs[:, :width]
    k = xs[:, width:2 * width]
    v = xs[:, 2 * width:3 * width]
    ones_bd = _head_ones(RW_GROUP, RW_HEAD_DIM)
    kk = k * kk_ref[...]
    kk = kk * lax.rsqrt(jnp.maximum(_head_sums(kk * kk, ones_bd), 1e-24))
    k = k * (1.0 + (a - 1.0) * ka_ref[...])
    bonus = _head_sums(r * k * rk_ref[...], ones_bd) * v

    r_out[0] = r
    lw_out[0] = log_decay
    k_out[0] = k
    v_out[0] = v
    kk_out[0] = kk
    kb_out[0] = kk * a
    g_out[0] = g
    bonus_out[0] = bonus


def _rw_prep(rw, mu, w0, w_w2, a0, w_a2, w_g2, k_k, k_a, r_k, *, bt=128):
    b, t, cols = rw.shape
    width = (cols - RW_LORA_COLS) // 3
    bt = min(bt, t)

    def padded(w2, first_row):
        full = jnp.zeros((RW_LORA_COLS, width), F32)
        return lax.dynamic_update_slice(full, w2.astype(F32), (first_row, 0)).astype(BF16)

    w2w = padded(w_w2, 0)
    w2a = padded(w_a2, RW_DECAY_RANK)
    w2g = padded(w_g2, RW_DECAY_RANK + RW_AAA_RANK)

    def vec(p):
        return p.reshape(1, -1).astype(F32)

    def vspec(n):
        return pl.BlockSpec((1, n), lambda bi, i: (0, 0))

    wspec = pl.BlockSpec((RW_LORA_COLS, width), lambda bi, i: (0, 0))
    ospec = pl.BlockSpec((1, bt, width), lambda bi, i: (bi, i, 0))
    return pl.pallas_call(
        functools.partial(_rw_prep_kernel, width=width),
        out_shape=[jax.ShapeDtypeStruct((b, t, width), F32)] * 8,
        grid=(b, t // bt),
        in_specs=[pl.BlockSpec((1, bt, cols), lambda bi, i: (bi, i, 0)),
                  pl.BlockSpec((1, 8, cols), lambda bi, i: (bi, jnp.maximum(i * (bt // 8) - 1, 0), 0)),
                  vspec(cols), vspec(width), vspec(width), vspec(width), vspec(width), vspec(width),
                  wspec, wspec, wspec],
        out_specs=[ospec] * 8,
        compiler_params=_cparams("parallel", "arbitrary"),
        name="rwkv7_prep",
    )(rw, rw, vec(mu), vec(w0), vec(a0), vec(k_k), vec(k_a), vec(r_k), w2w, w2a, w2g)


def _tile_rows(x, n):
    return jnp.concatenate([x] * n, axis=0)


def _rw_rec_kernel(r_ref, lw_ref, k_ref, v_ref, kk_ref, kb_ref, g_ref, bonus_ref,
                   lnw_ref, lnb_ref, o_ref, ht_ref, *, chunk, groups, inv_parts):
    nh, hd, gw = RW_GROUP_HEADS, RW_HEAD_DIM, RW_GROUP
    wide = nh * chunk
    tb = r_ref.shape[1]

    @pl.when(pl.program_id(2) == 0)
    def _():
        ht_ref[...] = jnp.zeros_like(ht_ref)

    def iota(shape, d):
        return lax.broadcasted_iota(jnp.int32, shape, d)

    ltri = jnp.where(iota((chunk, chunk), 1) <= iota((chunk, chunk), 0), 1.0, 0.0).astype(BF16)
    row_cw = iota((chunk, wide), 0)
    pos_cw = iota((chunk, wide), 1) % chunk
    strict = pos_cw < row_cw
    incl = pos_cw <= row_cw
    eye_t = jnp.where(pos_cw == row_cw, 1.0, 0.0)
    off_diag = []
    s = 1
    while s < chunk:
        off_diag.append(((row_cw // (2 * s)) == (pos_cw // (2 * s)))
                        & ((row_cw % (2 * s)) >= s) & ((pos_cw % (2 * s)) < s))
        s *= 2
    bd_wk = (iota((wide, gw), 0) // chunk) == (iota((wide, gw), 1) // hd)
    bd_ww = (iota((wide, wide), 0) // chunk) == (iota((wide, wide), 1) // chunk)
    head_mask = (iota((gw, gw), 0) // hd) == (iota((gw, gw), 1) // hd)
    ones_bd = _head_ones(gw, hd)

    def expand_k(x):
        return jnp.where(bd_wk, _tile_rows(x, nh), 0.0).astype(BF16)

    def expand_w(x):
        return jnp.where(bd_ww, _tile_rows(x, nh), 0.0)

    def mm(a, b):
        if inv_parts == 1:
            return _dot(a.astype(BF16), b.astype(BF16))
        ah, al = _split2(a)
        bh, bl = _split2(b)
        return _dot(ah, bh) + (_dot(ah, bl) + _dot(al, bh))

    def step(c, _):
        rows = pl.ds(pl.multiple_of(c * chunk, chunk), chunk)
        gs = range(groups)
        lanes = [slice(gi * gw, (gi + 1) * gw) for gi in gs]
        lw = lw_ref[0, rows, :]
        gam = _dot_exact_rhs01(ltri, lw, parts=3)
        gam_end = gam[chunk - 1:chunk, :]
        e_neg = jnp.exp(-gam)
        e_suf = jnp.exp(gam_end - gam)
        k_all = k_ref[0, rows, :]
        kb_all = kb_ref[0, rows, :]
        v_all = v_ref[0, rows, :]
        a_t = -kk_ref[0, rows, :] * jnp.exp(gam - lw)
        r_t = r_ref[0, rows, :] * jnp.exp(gam)
        b_t = kb_all * e_neg
        k_t = k_all * e_neg
        b_h = kb_all * e_suf
        k_h = k_all * e_suf
        h_decay = jnp.exp(gam_end)

        lhs = [jnp.concatenate([a_t[:, l], r_t[:, l]], axis=0).astype(BF16) for l in lanes]
        sb = [_dot_nt(lhs[g], expand_k(b_t[:, lanes[g]])) for g in gs]
        sk = [_dot_nt(lhs[g], expand_k(k_t[:, lanes[g]])) for g in gs]
        a_ab = [jnp.where(strict, s[:chunk], 0.0) for s in sb]
        a_rb = [jnp.where(incl, s[chunk:], 0.0) for s in sb]
        a_ak = [jnp.where(strict, s[:chunk], 0.0) for s in sk]
        a_rk = [jnp.where(incl, s[chunk:], 0.0) for s in sk]

        hts = [ht_ref[g] for g in gs]
        from_state = [_dot_nt(lhs[g], hts[g].astype(BF16)) for g in gs]
        from_v = [_dot(jnp.concatenate([a_ak[g], a_rk[g]], axis=0).astype(BF16),
                       expand_k(v_all[:, lanes[g]])) for g in gs]

        p_acc = [eye_t + jnp.where(off_diag[0], a, 0.0) for a in a_ab]
        for li in range(1, len(off_diag)):
            left = [mm(p_acc[g], expand_w(jnp.where(off_diag[li], a_ab[g], 0.0))) for g in gs]
            p_acc = [p_acc[g] + mm(left[g], expand_w(p_acc[g])) for g in gs]

        u = [_dot(p_acc[g].astype(BF16), expand_k(from_state[g][:chunk] + from_v[g][:chunk]))
             for g in gs]
        y = [from_state[g][chunk:] + from_v[g][chunk:] + _dot(a_rb[g].astype(BF16), expand_k(u[g]))
             for g in gs]
        upd = [_dot_tn(jnp.concatenate([u[g], v_all[:, lanes[g]]], axis=0).astype(BF16),
                       jnp.concatenate([b_h[:, lanes[g]], k_h[:, lanes[g]]], axis=0).astype(BF16))
               for g in gs]
        for g in gs:
            ht_ref[g] = hts[g] * h_decay[:, lanes[g]] + jnp.where(head_mask, upd[g], 0.0)

        mean = [_head_sums(y[g], ones_bd) * (1.0 / hd) for g in gs]
        yc = [y[g] - mean[g] for g in gs]
        var = [_head_sums(yc[g] * yc[g], ones_bd) * (1.0 / hd) for g in gs]
        for g in gs:
            l = lanes[g]
            yn = yc[g] * lax.rsqrt(var[g] + RW_GN_EPS) * lnw_ref[:, l] + lnb_ref[:, l]
            out = (yn + bonus_ref[0, rows, l]) * g_ref[0, rows, l]
            o_ref[0, rows, l] = out.astype(o_ref.dtype)
        return 0

    lax.fori_loop(0, tb // chunk, step, 0)


def _rw_recurrence(r, lw, k, v, kk, kb, g, bonus, lnx_w, lnx_b, *, chunk=64, groups=8, tb=256,
                   inv_parts=1):
    b, t, width = r.shape
    tb = min(tb, t)
    bw = groups * RW_GROUP
    spec = pl.BlockSpec((1, tb, bw), lambda bi, gi, ti: (bi, ti, gi))
    vspec = pl.BlockSpec((1, bw), lambda bi, gi, ti: (0, gi))
    return pl.pallas_call(
        functools.partial(_rw_rec_kernel, chunk=chunk, groups=groups, inv_parts=inv_parts),
        out_shape=jax.ShapeDtypeStruct((b, t, width), BF16),
        grid=(b, width // bw, t // tb),
        in_specs=[spec] * 8 + [vspec, vspec],
        out_specs=spec,
        scratch_shapes=[pltpu.VMEM((groups, RW_GROUP, RW_GROUP), F32)],
        compiler_params=_cparams("parallel", "parallel", "arbitrary"),
        name="rwkv7_recurrence",
    )(r, lw, k, v, kk, kb, g, bonus,
      lnx_w.reshape(1, width).astype(F32), lnx_b.reshape(1, width).astype(F32))


def rwkv7(rw, mu, w0, w_w2, a0, w_a2, w_g2, k_k, k_a, r_k, lnx_w, lnx_b, **rec_kw):
    parts = _rw_prep(rw, mu, w0, w_w2, a0, w_a2, w_g2, k_k, k_a, r_k)
    return _rw_recurrence(*parts, lnx_w, lnx_b, **rec_kw)


def kernel(x, norm1_g, w_in, sb_norm_g, hg_lb_param, hg_norm_g, rw_mu, rw_w0, rw_w_w2, rw_a0,
           rw_w_a2, rw_w_g2, rw_k_k, rw_k_a, rw_r_k, rw_lnx_w, rw_lnx_b, w_out, norm2_g,
           w_ff_in, w_ff_out, final_g):
    b, t, d = x.shape
    depth = norm1_g.shape[0]
    sb_w = sb_norm_g.shape[1]
    hg_w = hg_lb_param.shape[1]
    sb_cols, hg_cols = 3 * sb_w, 4 * hg_w
    xf = x.reshape(b * t, d)
    for l in range(depth):
        h = rmsnorm(xf, norm1_g[l], BF16)
        w = w_in[l]
        w_sb = w[:, :sb_cols].astype(BF16)
        w_hg = w[:, sb_cols:sb_cols + hg_cols].astype(BF16)
        w_rw = w[:, sb_cols + hg_cols:].astype(BF16)
        p_sb = matmul(h, w_sb, bm=1024, bn=1024, name="proj_sb").reshape(b, t, -1)
        p_hg = matmul(h, w_hg, bm=1024, bn=1024, name="proj_hg").reshape(b, t, -1)
        p_rw = matmul(h, w_rw, bm=1024, bn=1280, name="proj_rw").reshape(b, t, -1)
        o_sb = sb_attention(p_sb, sb_norm_g[l])
        o_hg = hgrn2(p_hg, hg_lb_param, hg_norm_g[l], layer=l)
        o_rw = rwkv7(p_rw, rw_mu[l], rw_w0[l], rw_w_w2[l], rw_a0[l], rw_w_a2[l], rw_w_g2[l],
                     rw_k_k[l], rw_k_a[l], rw_r_k[l], rw_lnx_w[l], rw_lnx_b[l])
        mix = jnp.concatenate([o_sb, o_hg, o_rw], axis=-1).reshape(b * t, d)
        xf = matmul(mix, w_out[l].astype(BF16), bm=1024, bn=1024, residual=xf, name="out_proj")
        h = rmsnorm(xf, norm2_g[l], BF16)
        ff = matmul(h, w_ff_in[l].astype(BF16), bm=1024, bn=1024, relu2=True, out_dtype=BF16,
                    name="ff_in")
        xf = matmul(ff, w_ff_out[l].astype(BF16), bm=1024, bn=1024, bk=2048, residual=xf,
                    name="ff_out")
    return rmsnorm(xf, final_g, F32).reshape(b, t, d)
```

```python
import functools
import math

import jax
import jax.numpy as jnp
import numpy as np
from jax import lax
from jax.experimental import pallas as pl
from jax.experimental.pallas import tpu as pltpu

F32 = jnp.float32
BF16 = jnp.bfloat16

NORM_EPS = 1e-5
V7X_VMEM_BYTES = 64 * 1024 * 1024
VMEM_LIMIT = 56 * 1024 * 1024
LANES = 128

SB_HEAD_DIM = 128
SB_LOG_WEIGHT_CUTOFF = -90.0
HG_HEAD_DIM = 128
HG_MIN_F = 1e-30
RW_HEAD_DIM = 64
RW_GN_EPS = 64e-5
RW_DECAY_RANK = 96
RW_AAA_RANK = 96
RW_GATE_RANK = 64
RW_LORA_COLS = RW_DECAY_RANK + RW_AAA_RANK + RW_GATE_RANK


def _cparams(*sem):
    return pltpu.CompilerParams(dimension_semantics=sem, vmem_limit_bytes=VMEM_LIMIT)


def _dot(a, b):
    return jnp.dot(a, b, preferred_element_type=F32)


def _dot_nt(a, b):
    return lax.dot_general(a, b, (((1,), (1,)), ((), ())), preferred_element_type=F32)


def _dot_tn(a, b):
    return lax.dot_general(a, b, (((0,), (0,)), ((), ())), preferred_element_type=F32)


def _split2(x):
    hi = x.astype(BF16)
    lo = (x - hi.astype(F32)).astype(BF16)
    return hi, lo


def _split3(x):
    hi = x.astype(BF16)
    r1 = x - hi.astype(F32)
    mid = r1.astype(BF16)
    lo = (r1 - mid.astype(F32)).astype(BF16)
    return hi, mid, lo


def _dot_exact_rhs01(a01, x, parts=3):
    ps = _split3(x) if parts == 3 else _split2(x)
    out = _dot(a01, ps[0])
    for p in ps[1:]:
        out = out + _dot(a01, p)
    return out


def _dot_exact_lhs(x, b01, parts=2):
    ps = _split3(x) if parts == 3 else _split2(x)
    out = _dot(ps[0], b01)
    for p in ps[1:]:
        out = out + _dot(p, b01)
    return out


def _rmsnorm_kernel(x_ref, g_ref, o_ref):
    x = x_ref[...]
    ms = jnp.mean(x * x, axis=-1, keepdims=True)
    o_ref[...] = (x * lax.rsqrt(ms + NORM_EPS) * g_ref[...]).astype(o_ref.dtype)


def rmsnorm(x, g, out_dtype, bm=512):
    m, d = x.shape
    bm = min(bm, m)
    return pl.pallas_call(
        _rmsnorm_kernel,
        out_shape=jax.ShapeDtypeStruct((m, d), out_dtype),
        grid=(m // bm,),
        in_specs=[pl.BlockSpec((bm, d), lambda i: (i, 0)),
                  pl.BlockSpec((1, d), lambda i: (0, 0))],
        out_specs=pl.BlockSpec((bm, d), lambda i: (i, 0)),
        compiler_params=_cparams("parallel"),
        name="rmsnorm",
    )(x, g.reshape(1, d).astype(F32))


def _matmul_kernel(*refs, n_lhs, nk, relu2, has_res):
    a_refs = refs[:n_lhs]
    b_ref = refs[n_lhs]
    r_ref = refs[n_lhs + 1] if has_res else None
    o_ref = refs[n_lhs + 1 + has_res]
    rest = refs[n_lhs + 2 + has_res:]

    def product():
        acc, k0 = None, 0
        for a_ref in a_refs:
            kw = a_ref.shape[1]
            part = _dot(a_ref[...], b_ref[k0:k0 + kw, :])
            acc = part if acc is None else acc + part
            k0 += kw
        return acc

    def finish(acc):
        if relu2:
            acc = jnp.square(jnp.maximum(acc, 0.0))
        if has_res:
            acc = acc + r_ref[...]
        o_ref[...] = acc.astype(o_ref.dtype)

    if nk == 1:
        finish(product())
        return

    acc_ref = rest[0]
    k = pl.program_id(2)

    @pl.when(k == 0)
    def _():
        acc_ref[...] = product()

    @pl.when(k > 0)
    def _():
        acc_ref[...] += product()

    @pl.when(k == nk - 1)
    def _():
        finish(acc_ref[...])


def matmul(a, b, *, bm, bn, bk=None, layer=0, col0=0, n=None, relu2=False, residual=None,
           out_dtype=F32, name="matmul"):
    a_list = list(a) if isinstance(a, (list, tuple)) else [a]
    m = a_list[0].shape[0]
    kdim = sum(x.shape[1] for x in a_list)
    n = b.shape[2] - col0 if n is None else n
    bm, bn = min(bm, m), min(bn, n)
    bk = kdim if bk is None else min(bk, kdim)
    assert b.shape[1] == kdim and m % bm == 0 and n % bn == 0 and kdim % bk == 0 and col0 % bn == 0
    nk = kdim // bk
    assert nk == 1 or len(a_list) == 1
    has_res = residual is not None
    jb0 = col0 // bn
    if len(a_list) == 1:
        in_specs = [pl.BlockSpec((bm, bk), lambda i, j, k: (i, k))]
    else:
        in_specs = [pl.BlockSpec((bm, x.shape[1]), lambda i, j, k: (i, 0)) for x in a_list]
    in_specs.append(pl.BlockSpec((None, bk, bn), lambda i, j, k: (layer, k, jb0 + j)))
    args = a_list + [b]
    if has_res:
        in_specs.append(pl.BlockSpec((bm, bn), lambda i, j, k: (i, j)))
        args.append(residual)
    return pl.pallas_call(
        functools.partial(_matmul_kernel, n_lhs=len(a_list), nk=nk, relu2=relu2, has_res=has_res),
        out_shape=jax.ShapeDtypeStruct((m, n), out_dtype),
        grid=(m // bm, n // bn, nk),
        in_specs=in_specs,
        out_specs=pl.BlockSpec((bm, bn), lambda i, j, k: (i, j)),
        scratch_shapes=[pltpu.VMEM((bm, bn), F32)] if nk > 1 else [],
        compiler_params=_cparams("parallel", "parallel", "arbitrary"),
        name=name,
    )(*args)


def _sb_kernel(q_ref, k_ref, v_ref, g_ref, o_ref, *, blk, heads):
    i = pl.program_id(2)
    scale = SB_HEAD_DIM ** -0.5
    hd = SB_HEAD_DIM
    qs = [q_ref[0, :, h * hd:(h + 1) * hd].astype(BF16) for h in range(heads)]
    row = lax.broadcasted_iota(jnp.int32, (blk, blk), 0)
    col = lax.broadcasted_iota(jnp.int32, (blk, blk), 1)
    before = col < row
    r2 = lax.broadcasted_iota(jnp.int32, (blk, 2 * blk), 0)
    c2 = lax.broadcasted_iota(jnp.int32, (blk, 2 * blk), 1)
    later = jnp.where((r2 > c2) | (c2 >= blk), 1.0, 0.0).astype(BF16)

    def block(j, carry, masked):
        start = pl.multiple_of(j * blk, blk)
        hs = range(heads)
        kbs = [k_ref[0, pl.ds(start, blk), h * hd:(h + 1) * hd].astype(BF16) for h in hs]
        vbs = [v_ref[0, pl.ds(start, blk), h * hd:(h + 1) * hd].astype(BF16) for h in hs]
        zs = [_dot_nt(qs[h], kbs[h]) * scale for h in hs]
        log_betas = [jnp.minimum(z, 0.0) - jnp.log1p(jnp.exp(-jnp.abs(z))) for z in zs]
        log_keeps = [lb - z for lb, z in zip(log_betas, zs)]
        if masked:
            log_keeps = [jnp.where(before, lk, 0.0) for lk in log_keeps]
        css = [_dot_exact_lhs(lk, later, parts=2) for lk in log_keeps]
        ws = [jnp.exp(log_betas[h] + carry[2 * h + 1] + css[h][:, :blk]) for h in hs]
        if masked:
            ws = [jnp.where(before, w, 0.0) for w in ws]
        out = []
        for h in hs:
            out += [carry[2 * h] + _dot(ws[h].astype(BF16), vbs[h]), carry[2 * h + 1] + css[h][:, blk:]]
        return tuple(out)

    carry = (jnp.zeros((blk, hd), F32), jnp.zeros((blk, blk), F32)) * heads
    carry = block(i, carry, True)

    def alive(state):
        j = state[0]
        top = state[2]
        for h in range(1, heads):
            top = jnp.maximum(top, state[2 + 2 * h])
        return (j >= 0) & (jnp.max(top) > SB_LOG_WEIGHT_CUTOFF)

    def body(state):
        j = state[0]
        return (j - 1,) + block(j, state[1:], False)

    state = lax.while_loop(alive, body, (i - 1,) + carry)
    for h in range(heads):
        acc = state[1 + 2 * h]
        ms = jnp.mean(acc * acc, axis=-1, keepdims=True)
        o_ref[0, :, h * hd:(h + 1) * hd] = (
            acc * lax.rsqrt(ms + NORM_EPS) * g_ref[:, h * hd:(h + 1) * hd]).astype(o_ref.dtype)


def sb_attention(proj_sb, norm_g, *, blk=128, heads=8):
    b, t, w3 = proj_sb.shape
    w = w3 // 3
    bw = heads * SB_HEAD_DIM
    nh = w // bw
    blk = min(blk, t)
    return pl.pallas_call(
        functools.partial(_sb_kernel, blk=blk, heads=heads),
        out_shape=jax.ShapeDtypeStruct((b, t, w), BF16),
        grid=(b, nh, t // blk),
        in_specs=[pl.BlockSpec((1, blk, bw), lambda bi, h, i: (bi, i, h)),
                  pl.BlockSpec((1, t, bw), lambda bi, h, i: (bi, 0, nh + h)),
                  pl.BlockSpec((1, t, bw), lambda bi, h, i: (bi, 0, 2 * nh + h)),
                  pl.BlockSpec((1, bw), lambda bi, h, i: (0, h))],
        out_specs=pl.BlockSpec((1, blk, bw), lambda bi, h, i: (bi, i, h)),
        compiler_params=_cparams("parallel", "parallel", "arbitrary"),
        name="sb_attention",
    )(proj_sb, proj_sb, proj_sb, norm_g.reshape(1, w).astype(F32))


def _hg_level_halves(chunk):
    halves = []
    m = chunk // 2
    while m >= 1:
        halves.append(m)
        m //= 2
    return halves


def _hg_decay_selectors(chunk):
    t = lax.broadcasted_iota(jnp.int32, (chunk, chunk), 0)
    j = lax.broadcasted_iota(jnp.int32, (chunk, chunk), 1)
    mats = [j <= t, j > t]
    for m in _hg_level_halves(chunk):
        mid = (t // (2 * m)) * (2 * m) + m - 1
        upper = (t % (2 * m)) >= m
        mats.append((upper & (j > mid) & (j <= t)) | (~upper & (j > t) & (j <= mid)))
    return jnp.concatenate([jnp.where(mm, 1.0, 0.0).astype(BF16) for mm in mats], axis=0)


def _hg_kernel(q_ref, f_ref, i_ref, g_ref, lbp_ref, ng_ref, o_ref, st_ref, *, chunk, layer, heads):
    hd = HG_HEAD_DIM
    tb = q_ref.shape[1]

    @pl.when(pl.program_id(2) == 0)
    def _():
        st_ref[...] = jnp.zeros_like(st_ref)

    halves = _hg_level_halves(chunk)
    sel = _hg_decay_selectors(chunk)

    p = lbp_ref[...]
    e = jnp.exp(p - jnp.max(p, axis=0, keepdims=True))
    probs = e / jnp.sum(e, axis=0, keepdims=True)
    lb = jnp.zeros((1, heads * hd), F32)
    for l in range(1, layer + 1):
        lb = lb + probs[l:l + 1, :]

    trow = lax.broadcasted_iota(jnp.int32, (chunk, heads * hd), 0)
    ts = lax.broadcasted_iota(jnp.int32, (chunk, chunk), 0)
    ss = lax.broadcasted_iota(jnp.int32, (chunk, chunk), 1)
    ng = ng_ref[...]

    def step(c, _):
        start = pl.multiple_of(c * chunk, chunk)
        rows = pl.ds(start, chunk)
        q2 = q_ref[0, rows, :]
        fr = f_ref[0, rows, :]
        v2 = i_ref[0, rows, :]
        gate = g_ref[0, rows, :]
        f = lb + (1.0 - lb) * jax.nn.sigmoid(fr)
        log_f = jnp.log(jnp.maximum(f, HG_MIN_F))
        key2 = (1.0 - lb) * jax.nn.sigmoid(-fr)
        ex2 = jnp.exp(_dot_exact_rhs01(sel, log_f, parts=2))
        silu = gate * jax.nn.sigmoid(gate)

        hs = range(heads)
        lanes = [slice(h * hd, (h + 1) * hd) for h in hs]
        states = [st_ref[h] for h in hs]
        vbs = [v2[:, l].astype(BF16) for l in lanes]
        scores = [None] * heads
        for li, m in enumerate(halves):
            el = ex2[(2 + li) * chunk:(3 + li) * chunk]
            upper = (trow % (2 * m)) >= m
            ql = jnp.where(upper, q2 * el, 0.0).astype(BF16)
            kl = jnp.where(upper, 0.0, key2 * el).astype(BF16)
            for h in hs:
                sl = _dot_nt(ql[:, lanes[h]], kl[:, lanes[h]])
                if 2 * m < chunk:
                    sl = jnp.where((ts // (2 * m)) == (ss // (2 * m)), sl, 0.0)
                scores[h] = sl if scores[h] is None else scores[h] + sl
        q_pre = (q2 * ex2[0:chunk]).astype(BF16)
        k_suf = (key2 * ex2[chunk:2 * chunk]).astype(BF16)
        qk = q2 * key2
        inter = [_dot_nt(q_pre[:, lanes[h]], states[h].astype(BF16)) for h in hs]
        intra = [_dot(scores[h].astype(BF16), vbs[h]) for h in hs]
        upd = [_dot_tn(vbs[h], k_suf[:, lanes[h]]) for h in hs]
        decay_all = ex2[chunk - 1:chunk, :]
        for h in hs:
            l = lanes[h]
            st_ref[h] = states[h] * decay_all[:, l] + upd[h]
            diag = jnp.sum(qk[:, l], axis=-1, keepdims=True)
            o = intra[h] + diag * v2[:, l] + inter[h]
            ms = jnp.mean(o * o, axis=-1, keepdims=True)
            o = o * lax.rsqrt(ms + NORM_EPS) * ng
            o_ref[0, rows, l] = (o * silu[:, l]).astype(o_ref.dtype)
        return 0

    lax.fori_loop(0, tb // chunk, step, 0)


def hgrn2(proj_hg, lb_param, norm_g, *, layer, chunk=128, heads=4, tb=512):
    b, t, w4 = proj_hg.shape
    w = w4 // 4
    bw = heads * HG_HEAD_DIM
    nh = w // bw
    tb = min(tb, t)
    chunk = min(chunk, tb)
    depth = lb_param.shape[0]

    def col(g):
        return pl.BlockSpec((1, tb, bw), lambda bi, h, ti: (bi, ti, g * nh + h))

    return pl.pallas_call(
        functools.partial(_hg_kernel, chunk=chunk, layer=layer, heads=heads),
        out_shape=jax.ShapeDtypeStruct((b, t, w), BF16),
        grid=(b, nh, t // tb),
        in_specs=[col(0), col(1), col(2), col(3),
                  pl.BlockSpec((depth, bw), lambda bi, h, ti: (0, h)),
                  pl.BlockSpec((1, HG_HEAD_DIM), lambda bi, h, ti: (0, 0))],
        out_specs=pl.BlockSpec((1, tb, bw), lambda bi, h, ti: (bi, ti, h)),
        scratch_shapes=[pltpu.VMEM((heads, HG_HEAD_DIM, HG_HEAD_DIM), F32)],
        compiler_params=_cparams("parallel", "parallel", "arbitrary"),
        name="hgrn2",
    )(proj_hg, proj_hg, proj_hg, proj_hg, lb_param.astype(F32),
      norm_g.reshape(1, HG_HEAD_DIM).astype(F32))


RW_GROUP_HEADS = 4
RW_GROUP = RW_GROUP_HEADS * RW_HEAD_DIM


def _head_ones(n, head):
    r = lax.broadcasted_iota(jnp.int32, (n, n), 0)
    c = lax.broadcasted_iota(jnp.int32, (n, n), 1)
    return jnp.where((r // head) == (c // head), 1.0, 0.0).astype(BF16)


def _head_sums(x, ones_bd):
    n = ones_bd.shape[0]
    outs = [_dot_exact_lhs(x[:, s:s + n], ones_bd, parts=2) for s in range(0, x.shape[1], n)]
    return outs[0] if len(outs) == 1 else jnp.concatenate(outs, axis=1)


def _softplus(y):
    return jnp.maximum(y, 0.0) + jnp.log1p(jnp.exp(-jnp.abs(y)))


def _rw_prep_kernel(x_ref, xp_ref, lo_ref, lop_ref, mu_ref, mulo_ref, w0_ref, a0_ref, kk_ref, ka_ref,
                    rk_ref, w2w_ref, w2a_ref, w2g_ref,
                    r_out, lw_out, k_out, v_out, kk_out, kb_out, g_out, bonus_out, *, width):
    i = pl.program_id(1)

    def token_shift(cur_ref, prev_ref, mix_ref):
        x = cur_ref[0]
        prev_row = jnp.where(i == 0, 0.0, prev_ref[0, 7:8, :])
        row = lax.broadcasted_iota(jnp.int32, x.shape, 0)
        prev = jnp.where(row == 0, prev_row, pltpu.roll(x, 1, axis=0))
        return x + mix_ref[...] * (prev - x)

    xs = token_shift(x_ref, xp_ref, mu_ref)
    lora = token_shift(lo_ref, lop_ref, mulo_ref)
    w_log = -_softplus(-(w0_ref[...] + _dot(jnp.tanh(lora).astype(BF16), w2w_ref[...]))) - 0.5
    log_decay = -jnp.exp(w_log)
    a = jax.nn.sigmoid(a0_ref[...] + _dot(lora.astype(BF16), w2a_ref[...]))
    g = _dot(jax.nn.sigmoid(lora).astype(BF16), w2g_ref[...])

    r = xs[:, :width]
    k = xs[:, width:2 * width]
    v = xs[:, 2 * width:3 * width]
    ones_bd = _head_ones(RW_GROUP, RW_HEAD_DIM)
    kk = k * kk_ref[...]
    kk = kk * lax.rsqrt(jnp.maximum(_head_sums(kk * kk, ones_bd), 1e-24))
    k = k * (1.0 + (a - 1.0) * ka_ref[...])
    bonus = _head_sums(r * k * rk_ref[...], ones_bd) * v

    r_out[0] = r
    lw_out[0] = log_decay
    k_out[0] = k
    v_out[0] = v
    kk_out[0] = kk
    kb_out[0] = kk * a
    g_out[0] = g
    bonus_out[0] = bonus


def _rw_prep(rkv, lora, mu, w0, w_w2, a0, w_a2, w_g2, k_k, k_a, r_k, *, bt=128):
    b, t, cols = rkv.shape
    width = cols // 3
    bt = min(bt, t)

    def padded(w2, first_row):
        full = jnp.zeros((RW_LORA_COLS, width), F32)
        return lax.dynamic_update_slice(full, w2.astype(F32), (first_row, 0)).astype(BF16)

    w2w = padded(w_w2, 0)
    w2a = padded(w_a2, RW_DECAY_RANK)
    w2g = padded(w_g2, RW_DECAY_RANK + RW_AAA_RANK)

    def vec(p):
        return p.reshape(1, -1).astype(F32)

    def vspec(n):
        return pl.BlockSpec((1, n), lambda bi, i: (0, 0))

    def cur(n):
        return pl.BlockSpec((1, bt, n), lambda bi, i: (bi, i, 0))

    def prev(n):
        return pl.BlockSpec((1, 8, n), lambda bi, i: (bi, jnp.maximum(i * (bt // 8) - 1, 0), 0))

    wspec = pl.BlockSpec((RW_LORA_COLS, width), lambda bi, i: (0, 0))
    ospec = pl.BlockSpec((1, bt, width), lambda bi, i: (bi, i, 0))
    return pl.pallas_call(
        functools.partial(_rw_prep_kernel, width=width),
        out_shape=[jax.ShapeDtypeStruct((b, t, width), F32)] * 8,
        grid=(b, t // bt),
        in_specs=[cur(cols), prev(cols), cur(RW_LORA_COLS), prev(RW_LORA_COLS),
                  vspec(cols), vspec(RW_LORA_COLS),
                  vspec(width), vspec(width), vspec(width), vspec(width), vspec(width),
                  wspec, wspec, wspec],
        out_specs=[ospec] * 8,
        compiler_params=_cparams("parallel", "arbitrary"),
        name="rwkv7_prep",
    )(rkv, rkv, lora, lora, vec(mu[:cols]), vec(mu[cols:]), vec(w0), vec(a0), vec(k_k), vec(k_a),
      vec(r_k), w2w, w2a, w2g)


def _tile_rows(x, n):
    return jnp.concatenate([x] * n, axis=0)


def _rw_rec_kernel(r_ref, lw_ref, k_ref, v_ref, kk_ref, kb_ref, g_ref, bonus_ref,
                   lnw_ref, lnb_ref, o_ref, ht_ref, *, chunk, groups, inv_parts):
    nh, hd, gw = RW_GROUP_HEADS, RW_HEAD_DIM, RW_GROUP
    wide = nh * chunk
    tb = r_ref.shape[1]

    @pl.when(pl.program_id(2) == 0)
    def _():
        ht_ref[...] = jnp.zeros_like(ht_ref)

    def iota(shape, d):
        return lax.broadcasted_iota(jnp.int32, shape, d)

    ltri = jnp.where(iota((chunk, chunk), 1) <= iota((chunk, chunk), 0), 1.0, 0.0).astype(BF16)
    row_cw = iota((chunk, wide), 0)
    pos_cw = iota((chunk, wide), 1) % chunk
    strict = pos_cw < row_cw
    incl = pos_cw <= row_cw
    eye_t = jnp.where(pos_cw == row_cw, 1.0, 0.0)
    off_diag = []
    s = 1
    while s < chunk:
        off_diag.append(((row_cw // (2 * s)) == (pos_cw // (2 * s)))
                        & ((row_cw % (2 * s)) >= s) & ((pos_cw % (2 * s)) < s))
        s *= 2
    bd_wk = (iota((wide, gw), 0) // chunk) == (iota((wide, gw), 1) // hd)
    bd_ww = (iota((wide, wide), 0) // chunk) == (iota((wide, wide), 1) // chunk)
    head_mask = (iota((gw, gw), 0) // hd) == (iota((gw, gw), 1) // hd)
    ones_bd = _head_ones(gw, hd)

    def expand_k(x):
        return jnp.where(bd_wk, _tile_rows(x, nh), 0.0).astype(BF16)

    def expand_w(x):
        return jnp.where(bd_ww, _tile_rows(x, nh), 0.0)

    def mm(a, b):
        if inv_parts == 1:
            return _dot(a.astype(BF16), b.astype(BF16))
        ah, al = _split2(a)
        bh, bl = _split2(b)
        return _dot(ah, bh) + (_dot(ah, bl) + _dot(al, bh))

    def step(c, _):
        rows = pl.ds(pl.multiple_of(c * chunk, chunk), chunk)
        gs = range(groups)
        lanes = [slice(gi * gw, (gi + 1) * gw) for gi in gs]
        lw = lw_ref[0, rows, :]
        gam = _dot_exact_rhs01(ltri, lw, parts=2)
        gam_end = gam[chunk - 1:chunk, :]
        e_neg = jnp.exp(-gam)
        e_suf = jnp.exp(gam_end - gam)
        k_all = k_ref[0, rows, :]
        kb_all = kb_ref[0, rows, :]
        v_all = v_ref[0, rows, :]
        a_t = -kk_ref[0, rows, :] * jnp.exp(gam - lw)
        r_t = r_ref[0, rows, :] * jnp.exp(gam)
        b_t = kb_all * e_neg
        k_t = k_all * e_neg
        b_h = kb_all * e_suf
        k_h = k_all * e_suf
        h_decay = jnp.exp(gam_end)

        lhs = [jnp.concatenate([a_t[:, l], r_t[:, l]], axis=0).astype(BF16) for l in lanes]
        sb = [_dot_nt(lhs[g], expand_k(b_t[:, lanes[g]])) for g in gs]
        sk = [_dot_nt(lhs[g], expand_k(k_t[:, lanes[g]])) for g in gs]
        a_ab = [jnp.where(strict, s[:chunk], 0.0) for s in sb]
        a_rb = [jnp.where(incl, s[chunk:], 0.0) for s in sb]
        a_ak = [jnp.where(strict, s[:chunk], 0.0) for s in sk]
        a_rk = [jnp.where(incl, s[chunk:], 0.0) for s in sk]

        hts = [ht_ref[g] for g in gs]
        from_state = [_dot_nt(lhs[g], hts[g].astype(BF16)) for g in gs]
        from_v = [_dot(jnp.concatenate([a_ak[g], a_rk[g]], axis=0).astype(BF16),
                       expand_k(v_all[:, lanes[g]])) for g in gs]

        p_acc = [eye_t + jnp.where(off_diag[0], a, 0.0) for a in a_ab]
        for li in range(1, len(off_diag)):
            left = [mm(p_acc[g], expand_w(jnp.where(off_diag[li], a_ab[g], 0.0))) for g in gs]
            p_acc = [p_acc[g] + mm(left[g], expand_w(p_acc[g])) for g in gs]

        u = [_dot(p_acc[g].astype(BF16), expand_k(from_state[g][:chunk] + from_v[g][:chunk]))
             for g in gs]
        y = [from_state[g][chunk:] + from_v[g][chunk:] + _dot(a_rb[g].astype(BF16), expand_k(u[g]))
             for g in gs]
        upd = [_dot_tn(jnp.concatenate([u[g], v_all[:, lanes[g]]], axis=0).astype(BF16),
                       jnp.concatenate([b_h[:, lanes[g]], k_h[:, lanes[g]]], axis=0).astype(BF16))
               for g in gs]
        for g in gs:
            ht_ref[g] = hts[g] * h_decay[:, lanes[g]] + jnp.where(head_mask, upd[g], 0.0)

        mean = [_head_sums(y[g], ones_bd) * (1.0 / hd) for g in gs]
        yc = [y[g] - mean[g] for g in gs]
        var = [_head_sums(yc[g] * yc[g], ones_bd) * (1.0 / hd) for g in gs]
        for g in gs:
            l = lanes[g]
            yn = yc[g] * lax.rsqrt(var[g] + RW_GN_EPS) * lnw_ref[:, l] + lnb_ref[:, l]
            out = (yn + bonus_ref[0, rows, l]) * g_ref[0, rows, l]
            o_ref[0, rows, l] = out.astype(o_ref.dtype)
        return 0

    lax.fori_loop(0, tb // chunk, step, 0)


def _rw_recurrence(r, lw, k, v, kk, kb, g, bonus, lnx_w, lnx_b, *, chunk=64, groups=8, tb=256,
                   inv_parts=1):
    b, t, width = r.shape
    tb = min(tb, t)
    bw = groups * RW_GROUP
    spec = pl.BlockSpec((1, tb, bw), lambda bi, gi, ti: (bi, ti, gi))
    vspec = pl.BlockSpec((1, bw), lambda bi, gi, ti: (0, gi))
    return pl.pallas_call(
        functools.partial(_rw_rec_kernel, chunk=chunk, groups=groups, inv_parts=inv_parts),
        out_shape=jax.ShapeDtypeStruct((b, t, width), BF16),
        grid=(b, width // bw, t // tb),
        in_specs=[spec] * 8 + [vspec, vspec],
        out_specs=spec,
        scratch_shapes=[pltpu.VMEM((groups, RW_GROUP, RW_GROUP), F32)],
        compiler_params=_cparams("parallel", "parallel", "arbitrary"),
        name="rwkv7_recurrence",
    )(r, lw, k, v, kk, kb, g, bonus,
      lnx_w.reshape(1, width).astype(F32), lnx_b.reshape(1, width).astype(F32))


def rwkv7(rkv, lora, mu, w0, w_w2, a0, w_a2, w_g2, k_k, k_a, r_k, lnx_w, lnx_b, **rec_kw):
    parts = _rw_prep(rkv, lora, mu, w0, w_w2, a0, w_a2, w_g2, k_k, k_a, r_k)
    return _rw_recurrence(*parts, lnx_w, lnx_b, **rec_kw)


def kernel(x, norm1_g, w_in, sb_norm_g, hg_lb_param, hg_norm_g, rw_mu, rw_w0, rw_w_w2, rw_a0,
           rw_w_a2, rw_w_g2, rw_k_k, rw_k_a, rw_r_k, rw_lnx_w, rw_lnx_b, w_out, norm2_g,
           w_ff_in, w_ff_out, final_g):
    b, t, d = x.shape
    depth = norm1_g.shape[0]
    sb_w = sb_norm_g.shape[1]
    hg_w = hg_lb_param.shape[1]
    rw_w = rw_lnx_w.shape[1]
    sb_cols, hg_cols, rkv_cols = 3 * sb_w, 4 * hg_w, 3 * rw_w
    m = b * t
    xf = x.reshape(m, d)
    w_in_b, w_out_b = w_in.astype(BF16), w_out.astype(BF16)
    w_ff_in_b, w_ff_out_b = w_ff_in.astype(BF16), w_ff_out.astype(BF16)
    for l in range(depth):
        h = rmsnorm(xf, norm1_g[l], BF16)
        proj = functools.partial(matmul, h, w_in_b, layer=l, bm=1024)
        p_sb = proj(bn=1024, col0=0, n=sb_cols, out_dtype=BF16, name="proj_sb").reshape(b, t, -1)
        p_hg = proj(bn=1024, col0=sb_cols, n=hg_cols, name="proj_hg").reshape(b, t, -1)
        p_rkv = proj(bn=1024, col0=sb_cols + hg_cols, n=rkv_cols, name="proj_rkv").reshape(b, t, -1)
        p_lora = proj(bn=RW_LORA_COLS, col0=sb_cols + hg_cols + rkv_cols, n=RW_LORA_COLS,
                      name="proj_lora").reshape(b, t, -1)
        o_sb = sb_attention(p_sb, sb_norm_g[l])
        o_hg = hgrn2(p_hg, hg_lb_param, hg_norm_g[l], layer=l)
        o_rw = rwkv7(p_rkv, p_lora, rw_mu[l], rw_w0[l], rw_w_w2[l], rw_a0[l], rw_w_a2[l],
                     rw_w_g2[l], rw_k_k[l], rw_k_a[l], rw_r_k[l], rw_lnx_w[l], rw_lnx_b[l])
        mix = [o.reshape(m, -1) for o in (o_sb, o_hg, o_rw)]
        xf = matmul(mix, w_out_b, layer=l, bm=1024, bn=1024, residual=xf, name="out_proj")
        h = rmsnorm(xf, norm2_g[l], BF16)
        ff = matmul(h, w_ff_in_b, layer=l, bm=1024, bn=1024, relu2=True, out_dtype=BF16,
                    name="ff_in")
        xf = matmul(ff, w_ff_out_b, layer=l, bm=1024, bn=1024, bk=2048, residual=xf,
                    name="ff_out")
    return rmsnorm(xf, final_g, F32).reshape(b, t, d)
```

```python
import functools
import math

import jax
import jax.numpy as jnp
import numpy as np
from jax import lax
from jax.experimental import pallas as pl
from jax.experimental.pallas import tpu as pltpu

F32 = jnp.float32
BF16 = jnp.bfloat16

NORM_EPS = 1e-5
V7X_VMEM_BYTES = 64 * 1024 * 1024
VMEM_LIMIT = 56 * 1024 * 1024
LANES = 128

SB_HEAD_DIM = 128
SB_LOG_WEIGHT_CUTOFF = -90.0
HG_HEAD_DIM = 128
HG_MIN_F = 1e-30
RW_HEAD_DIM = 64
RW_GN_EPS = 64e-5
RW_DECAY_RANK = 96
RW_AAA_RANK = 96
RW_GATE_RANK = 64
RW_LORA_COLS = RW_DECAY_RANK + RW_AAA_RANK + RW_GATE_RANK


def _cparams(*sem):
    return pltpu.CompilerParams(dimension_semantics=sem, vmem_limit_bytes=VMEM_LIMIT)


def _dot(a, b):
    return jnp.dot(a, b, preferred_element_type=F32)


def _dot_nt(a, b):
    return lax.dot_general(a, b, (((1,), (1,)), ((), ())), preferred_element_type=F32)


def _dot_tn(a, b):
    return lax.dot_general(a, b, (((0,), (0,)), ((), ())), preferred_element_type=F32)


def _split2(x):
    hi = x.astype(BF16)
    lo = (x - hi.astype(F32)).astype(BF16)
    return hi, lo


def _split3(x):
    hi = x.astype(BF16)
    r1 = x - hi.astype(F32)
    mid = r1.astype(BF16)
    lo = (r1 - mid.astype(F32)).astype(BF16)
    return hi, mid, lo


def _dot_exact_rhs01(a01, x, parts=3):
    ps = _split3(x) if parts == 3 else _split2(x)
    out = _dot(a01, ps[0])
    for p in ps[1:]:
        out = out + _dot(a01, p)
    return out


def _dot_exact_lhs(x, b01, parts=2):
    ps = _split3(x) if parts == 3 else _split2(x)
    out = _dot(ps[0], b01)
    for p in ps[1:]:
        out = out + _dot(p, b01)
    return out


def _rmsnorm_kernel(x_ref, g_ref, o_ref):
    x = x_ref[...]
    ms = jnp.mean(x * x, axis=-1, keepdims=True)
    o_ref[...] = (x * lax.rsqrt(ms + NORM_EPS) * g_ref[...]).astype(o_ref.dtype)


def rmsnorm(x, g, out_dtype, bm=512):
    m, d = x.shape
    bm = min(bm, m)
    return pl.pallas_call(
        _rmsnorm_kernel,
        out_shape=jax.ShapeDtypeStruct((m, d), out_dtype),
        grid=(m // bm,),
        in_specs=[pl.BlockSpec((bm, d), lambda i: (i, 0)),
                  pl.BlockSpec((1, d), lambda i: (0, 0))],
        out_specs=pl.BlockSpec((bm, d), lambda i: (i, 0)),
        compiler_params=_cparams("parallel"),
        name="rmsnorm",
    )(x, g.reshape(1, d).astype(F32))


def _matmul_kernel(*refs, n_lhs, nk, relu2, has_res, has_row_ss, emit_norm, norm_dim):
    refs = list(refs)
    a_refs = [refs.pop(0) for _ in range(n_lhs)]
    b_ref = refs.pop(0)
    r_ref = refs.pop(0) if has_res else None
    ss_in_ref = refs.pop(0) if has_row_ss else None
    gnext_ref = refs.pop(0) if emit_norm else None
    o_ref = refs.pop(0)
    xg_ref, ss_out_ref = (refs.pop(0), refs.pop(0)) if emit_norm else (None, None)
    rest = refs
    j = pl.program_id(1)

    def product():
        acc, k0 = None, 0
        for a_ref in a_refs:
            kw = a_ref.shape[1]
            part = _dot(a_ref[...], b_ref[k0:k0 + kw, :])
            acc = part if acc is None else acc + part
            k0 += kw
        return acc

    def finish(acc):
        if has_row_ss:
            acc = acc * lax.rsqrt(ss_in_ref[:, :1] * (1.0 / norm_dim) + NORM_EPS)
        if relu2:
            acc = jnp.square(jnp.maximum(acc, 0.0))
        if has_res:
            acc = acc + r_ref[...]
        o_ref[...] = acc.astype(o_ref.dtype)
        if emit_norm:
            xg_ref[...] = (acc * gnext_ref[...]).astype(xg_ref.dtype)
            part = jnp.broadcast_to(jnp.sum(acc * acc, axis=-1, keepdims=True), ss_out_ref.shape)

            @pl.when(j == 0)
            def _():
                ss_out_ref[...] = part

            @pl.when(j > 0)
            def _():
                ss_out_ref[...] += part

    if nk == 1:
        finish(product())
        return

    acc_ref = rest[0]
    k = pl.program_id(2)

    @pl.when(k == 0)
    def _():
        acc_ref[...] = product()

    @pl.when(k > 0)
    def _():
        acc_ref[...] += product()

    @pl.when(k == nk - 1)
    def _():
        finish(acc_ref[...])


def matmul(a, b, *, bm, bn, bk=None, layer=0, col0=0, n=None, relu2=False, residual=None,
           row_ss=None, next_norm_g=None, out_dtype=F32, name="matmul"):
    a_list = list(a) if isinstance(a, (list, tuple)) else [a]
    m = a_list[0].shape[0]
    kdim = sum(x.shape[1] for x in a_list)
    n = b.shape[2] - col0 if n is None else n
    bm, bn = min(bm, m), min(bn, n)
    bk = kdim if bk is None else min(bk, kdim)
    assert b.shape[1] == kdim and m % bm == 0 and n % bn == 0 and kdim % bk == 0 and col0 % bn == 0
    nk = kdim // bk
    assert nk == 1 or len(a_list) == 1
    has_res = residual is not None
    jb0 = col0 // bn
    if len(a_list) == 1:
        in_specs = [pl.BlockSpec((bm, bk), lambda i, j, k: (i, k))]
    else:
        in_specs = [pl.BlockSpec((bm, x.shape[1]), lambda i, j, k: (i, 0)) for x in a_list]
    in_specs.append(pl.BlockSpec((None, bk, bn), lambda i, j, k: (layer, k, jb0 + j)))
    args = a_list + [b]
    if has_res:
        in_specs.append(pl.BlockSpec((bm, bn), lambda i, j, k: (i, j)))
        args.append(residual)
    stat_spec = pl.BlockSpec((bm, LANES), lambda i, j, k: (i, 0))
    if row_ss is not None:
        in_specs.append(stat_spec)
        args.append(row_ss)
    emit_norm = next_norm_g is not None
    out_shape = jax.ShapeDtypeStruct((m, n), out_dtype)
    out_specs = pl.BlockSpec((bm, bn), lambda i, j, k: (i, j))
    if emit_norm:
        assert col0 == 0 and n == b.shape[2]
        in_specs.append(pl.BlockSpec((1, bn), lambda i, j, k: (0, j)))
        args.append(next_norm_g.reshape(1, n).astype(F32))
        out_shape = [out_shape, jax.ShapeDtypeStruct((m, n), BF16),
                     jax.ShapeDtypeStruct((m, LANES), F32)]
        out_specs = [out_specs, pl.BlockSpec((bm, bn), lambda i, j, k: (i, j)), stat_spec]
    return pl.pallas_call(
        functools.partial(_matmul_kernel, n_lhs=len(a_list), nk=nk, relu2=relu2, has_res=has_res,
                          has_row_ss=row_ss is not None, emit_norm=emit_norm, norm_dim=kdim),
        out_shape=out_shape,
        grid=(m // bm, n // bn, nk),
        in_specs=in_specs,
        out_specs=out_specs,
        scratch_shapes=[pltpu.VMEM((bm, bn), F32)] if nk > 1 else [],
        compiler_params=_cparams("parallel", "arbitrary" if emit_norm else "parallel", "arbitrary"),
        name=name,
    )(*args)


def _sb_kernel(q_ref, k_ref, v_ref, g_ref, o_ref, *, blk, heads):
    i = pl.program_id(2)
    scale = SB_HEAD_DIM ** -0.5
    hd = SB_HEAD_DIM
    qs = [q_ref[0, :, h * hd:(h + 1) * hd].astype(BF16) for h in range(heads)]
    row = lax.broadcasted_iota(jnp.int32, (blk, blk), 0)
    col = lax.broadcasted_iota(jnp.int32, (blk, blk), 1)
    before = col < row
    r2 = lax.broadcasted_iota(jnp.int32, (blk, 2 * blk), 0)
    c2 = lax.broadcasted_iota(jnp.int32, (blk, 2 * blk), 1)
    later = jnp.where((r2 > c2) | (c2 >= blk), 1.0, 0.0).astype(BF16)

    def block(j, carry, masked):
        start = pl.multiple_of(j * blk, blk)
        hs = range(heads)
        kbs = [k_ref[0, pl.ds(start, blk), h * hd:(h + 1) * hd].astype(BF16) for h in hs]
        vbs = [v_ref[0, pl.ds(start, blk), h * hd:(h + 1) * hd].astype(BF16) for h in hs]
        zs = [_dot_nt(qs[h], kbs[h]) * scale for h in hs]
        log_betas = [jnp.minimum(z, 0.0) - jnp.log1p(jnp.exp(-jnp.abs(z))) for z in zs]
        log_keeps = [lb - z for lb, z in zip(log_betas, zs)]
        if masked:
            log_keeps = [jnp.where(before, lk, 0.0) for lk in log_keeps]
        cs_all = _dot_exact_lhs(jnp.concatenate(log_keeps, axis=0), later, parts=2)
        css = [cs_all[h * blk:(h + 1) * blk] for h in hs]
        ws = [jnp.exp(log_betas[h] + carry[2 * h + 1] + css[h][:, :blk]) for h in hs]
        if masked:
            ws = [jnp.where(before, w, 0.0) for w in ws]
        out = []
        for h in hs:
            out += [carry[2 * h] + _dot(ws[h].astype(BF16), vbs[h]), carry[2 * h + 1] + css[h][:, blk:]]
        return tuple(out)

    carry = (jnp.zeros((blk, hd), F32), jnp.zeros((blk, blk), F32)) * heads
    carry = block(i, carry, True)

    def alive(state):
        j = state[0]
        top = state[2]
        for h in range(1, heads):
            top = jnp.maximum(top, state[2 + 2 * h])
        return (j >= 0) & (jnp.max(top) > SB_LOG_WEIGHT_CUTOFF)

    def body(state):
        j = state[0]
        return (j - 1,) + block(j, state[1:], False)

    state = lax.while_loop(alive, body, (i - 1,) + carry)
    for h in range(heads):
        acc = state[1 + 2 * h]
        ms = jnp.mean(acc * acc, axis=-1, keepdims=True)
        o_ref[0, :, h * hd:(h + 1) * hd] = (
            acc * lax.rsqrt(ms + NORM_EPS) * g_ref[:, h * hd:(h + 1) * hd]).astype(o_ref.dtype)


def sb_attention(proj_sb, norm_g, *, blk=128, heads=8):
    b, t, w3 = proj_sb.shape
    w = w3 // 3
    bw = heads * SB_HEAD_DIM
    nh = w // bw
    blk = min(blk, t)
    return pl.pallas_call(
        functools.partial(_sb_kernel, blk=blk, heads=heads),
        out_shape=jax.ShapeDtypeStruct((b, t, w), BF16),
        grid=(b, nh, t // blk),
        in_specs=[pl.BlockSpec((1, blk, bw), lambda bi, h, i: (bi, i, h)),
                  pl.BlockSpec((1, t, bw), lambda bi, h, i: (bi, 0, nh + h)),
                  pl.BlockSpec((1, t, bw), lambda bi, h, i: (bi, 0, 2 * nh + h)),
                  pl.BlockSpec((1, bw), lambda bi, h, i: (0, h))],
        out_specs=pl.BlockSpec((1, blk, bw), lambda bi, h, i: (bi, i, h)),
        compiler_params=_cparams("parallel", "parallel", "arbitrary"),
        name="sb_attention",
    )(proj_sb, proj_sb, proj_sb, norm_g.reshape(1, w).astype(F32))


def _hg_level_halves(chunk):
    halves = []
    m = chunk // 2
    while m >= 1:
        halves.append(m)
        m //= 2
    return halves


HG_ROW_TILE = 8


def _hg_decay_selectors(chunk):
    t = lax.broadcasted_iota(jnp.int32, (chunk, chunk), 0)
    j = lax.broadcasted_iota(jnp.int32, (chunk, chunk), 1)
    mats = [j <= t]
    for m in _hg_level_halves(chunk):
        if m >= HG_ROW_TILE:
            continue
        mid = (t // (2 * m)) * (2 * m) + m - 1
        upper = (t % (2 * m)) >= m
        mats.append((upper & (j > mid) & (j <= t)) | (~upper & (j > t) & (j <= mid)))
    return jnp.concatenate([jnp.where(mm, 1.0, 0.0).astype(BF16) for mm in mats], axis=0)


def _hg_level_decays(sums, chunk):
    b = sums[0:chunk]
    out, small = [], 0
    for m in _hg_level_halves(chunk):
        if m >= HG_ROW_TILE:
            mids = [jnp.broadcast_to(b[s + m - 1:s + m, :], (2 * m, b.shape[1]))
                    for s in range(0, chunk, 2 * m)]
            b_mid = mids[0] if len(mids) == 1 else jnp.concatenate(mids, axis=0)
            out.append(jnp.exp(-jnp.abs(b - b_mid)))
        else:
            small += 1
            out.append(jnp.exp(sums[small * chunk:(small + 1) * chunk]))
    return out


def _hg_kernel(q_ref, f_ref, i_ref, g_ref, lbp_ref, ng_ref, o_ref, st_ref, *, chunk, layer, heads):
    hd = HG_HEAD_DIM
    tb = q_ref.shape[1]

    @pl.when(pl.program_id(2) == 0)
    def _():
        st_ref[...] = jnp.zeros_like(st_ref)

    halves = _hg_level_halves(chunk)
    sel = _hg_decay_selectors(chunk)

    p = lbp_ref[...]
    e = jnp.exp(p - jnp.max(p, axis=0, keepdims=True))
    probs = e / jnp.sum(e, axis=0, keepdims=True)
    lb = jnp.zeros((1, heads * hd), F32)
    for l in range(1, layer + 1):
        lb = lb + probs[l:l + 1, :]

    trow = lax.broadcasted_iota(jnp.int32, (chunk, heads * hd), 0)
    ts = lax.broadcasted_iota(jnp.int32, (chunk, chunk), 0)
    ss = lax.broadcasted_iota(jnp.int32, (chunk, chunk), 1)
    ng = ng_ref[...]

    def step(c, _):
        start = pl.multiple_of(c * chunk, chunk)
        rows = pl.ds(start, chunk)
        q2 = q_ref[0, rows, :]
        fr = f_ref[0, rows, :]
        v2 = i_ref[0, rows, :]
        gate = g_ref[0, rows, :]
        f = lb + (1.0 - lb) * jax.nn.sigmoid(fr)
        log_f = jnp.log(jnp.maximum(f, HG_MIN_F))
        key2 = (1.0 - lb) * jax.nn.sigmoid(-fr)
        sums = _dot_exact_rhs01(sel, log_f, parts=2)
        b = sums[0:chunk]
        b_last = b[chunk - 1:chunk, :]
        level_decay = _hg_level_decays(sums, chunk)
        silu = gate * jax.nn.sigmoid(gate)

        hs = range(heads)
        lanes = [slice(h * hd, (h + 1) * hd) for h in hs]
        states = [st_ref[h] for h in hs]
        vbs = [v2[:, l].astype(BF16) for l in lanes]
        scores = [None] * heads
        for li, m in enumerate(halves):
            el = level_decay[li]
            upper = (trow % (2 * m)) >= m
            ql = jnp.where(upper, q2 * el, 0.0).astype(BF16)
            kl = jnp.where(upper, 0.0, key2 * el).astype(BF16)
            for h in hs:
                sl = _dot_nt(ql[:, lanes[h]], kl[:, lanes[h]])
                if 2 * m < chunk:
                    sl = jnp.where((ts // (2 * m)) == (ss // (2 * m)), sl, 0.0)
                scores[h] = sl if scores[h] is None else scores[h] + sl
        q_pre = (q2 * jnp.exp(b)).astype(BF16)
        k_suf = (key2 * jnp.exp(b_last - b)).astype(BF16)
        qk = q2 * key2
        inter = [_dot_nt(q_pre[:, lanes[h]], states[h].astype(BF16)) for h in hs]
        intra = [_dot(scores[h].astype(BF16), vbs[h]) for h in hs]
        upd = [_dot_tn(vbs[h], k_suf[:, lanes[h]]) for h in hs]
        decay_all = jnp.exp(b_last)
        for h in hs:
            l = lanes[h]
            st_ref[h] = states[h] * decay_all[:, l] + upd[h]
            diag = jnp.sum(qk[:, l], axis=-1, keepdims=True)
            o = intra[h] + diag * v2[:, l] + inter[h]
            ms = jnp.mean(o * o, axis=-1, keepdims=True)
            o = o * lax.rsqrt(ms + NORM_EPS) * ng
            o_ref[0, rows, l] = (o * silu[:, l]).astype(o_ref.dtype)
        return 0

    lax.fori_loop(0, tb // chunk, step, 0)


def hgrn2(proj_hg, lb_param, norm_g, *, layer, chunk=128, heads=4, tb=512):
    b, t, w4 = proj_hg.shape
    w = w4 // 4
    bw = heads * HG_HEAD_DIM
    nh = w // bw
    tb = min(tb, t)
    chunk = min(chunk, tb)
    depth = lb_param.shape[0]

    def col(g):
        return pl.BlockSpec((1, tb, bw), lambda bi, h, ti: (bi, ti, g * nh + h))

    return pl.pallas_call(
        functools.partial(_hg_kernel, chunk=chunk, layer=layer, heads=heads),
        out_shape=jax.ShapeDtypeStruct((b, t, w), BF16),
        grid=(b, nh, t // tb),
        in_specs=[col(0), col(1), col(2), col(3),
                  pl.BlockSpec((depth, bw), lambda bi, h, ti: (0, h)),
                  pl.BlockSpec((1, HG_HEAD_DIM), lambda bi, h, ti: (0, 0))],
        out_specs=pl.BlockSpec((1, tb, bw), lambda bi, h, ti: (bi, ti, h)),
        scratch_shapes=[pltpu.VMEM((heads, HG_HEAD_DIM, HG_HEAD_DIM), F32)],
        compiler_params=_cparams("parallel", "parallel", "arbitrary"),
        name="hgrn2",
    )(proj_hg, proj_hg, proj_hg, proj_hg, lb_param.astype(F32),
      norm_g.reshape(1, HG_HEAD_DIM).astype(F32))


RW_GROUP_HEADS = 4
RW_GROUP = RW_GROUP_HEADS * RW_HEAD_DIM


def _head_ones(n, head):
    r = lax.broadcasted_iota(jnp.int32, (n, n), 0)
    c = lax.broadcasted_iota(jnp.int32, (n, n), 1)
    return jnp.where((r // head) == (c // head), 1.0, 0.0).astype(BF16)


def _head_sums(x, ones_bd):
    n = ones_bd.shape[0]
    outs = [_dot_exact_lhs(x[:, s:s + n], ones_bd, parts=2) for s in range(0, x.shape[1], n)]
    return outs[0] if len(outs) == 1 else jnp.concatenate(outs, axis=1)


def _softplus(y):
    return jnp.maximum(y, 0.0) + jnp.log1p(jnp.exp(-jnp.abs(y)))


def _rw_prep_kernel(x_ref, xp_ref, lo_ref, lop_ref, mu_ref, mulo_ref, w0_ref, a0_ref, kk_ref, ka_ref,
                    rk_ref, w2w_ref, w2a_ref, w2g_ref,
                    r_out, lw_out, k_out, v_out, kk_out, kb_out, g_out, bonus_out, *, width):
    i = pl.program_id(1)

    def token_shift(cur_ref, prev_ref, mix_ref):
        x = cur_ref[0]
        prev_row = jnp.where(i == 0, 0.0, prev_ref[0, 7:8, :])
        row = lax.broadcasted_iota(jnp.int32, x.shape, 0)
        prev = jnp.where(row == 0, prev_row, pltpu.roll(x, 1, axis=0))
        return x + mix_ref[...] * (prev - x)

    xs = token_shift(x_ref, xp_ref, mu_ref)
    lora = token_shift(lo_ref, lop_ref, mulo_ref)
    w_log = -_softplus(-(w0_ref[...] + _dot(jnp.tanh(lora).astype(BF16), w2w_ref[...]))) - 0.5
    log_decay = -jnp.exp(w_log)
    a = jax.nn.sigmoid(a0_ref[...] + _dot(lora.astype(BF16), w2a_ref[...]))
    g = _dot(jax.nn.sigmoid(lora).astype(BF16), w2g_ref[...])

    r = xs[:, :width]
    k = xs[:, width:2 * width]
    v = xs[:, 2 * width:3 * width]
    ones_bd = _head_ones(RW_GROUP, RW_HEAD_DIM)
    kk = k * kk_ref[...]
    kk = kk * lax.rsqrt(jnp.maximum(_head_sums(kk * kk, ones_bd), 1e-24))
    k = k * (1.0 + (a - 1.0) * ka_ref[...])
    bonus = _head_sums(r * k * rk_ref[...], ones_bd) * v

    r_out[0] = r
    lw_out[0] = log_decay
    k_out[0] = k
    v_out[0] = v
    kk_out[0] = kk
    kb_out[0] = kk * a
    g_out[0] = g
    bonus_out[0] = bonus


def _rw_prep(rkv, lora, mu, w0, w_w2, a0, w_a2, w_g2, k_k, k_a, r_k, *, bt=128):
    b, t, cols = rkv.shape
    width = cols // 3
    bt = min(bt, t)

    def padded(w2, first_row):
        full = jnp.zeros((RW_LORA_COLS, width), F32)
        return lax.dynamic_update_slice(full, w2.astype(F32), (first_row, 0)).astype(BF16)

    w2w = padded(w_w2, 0)
    w2a = padded(w_a2, RW_DECAY_RANK)
    w2g = padded(w_g2, RW_DECAY_RANK + RW_AAA_RANK)

    def vec(p):
        return p.reshape(1, -1).astype(F32)

    def vspec(n):
        return pl.BlockSpec((1, n), lambda bi, i: (0, 0))

    def cur(n):
        return pl.BlockSpec((1, bt, n), lambda bi, i: (bi, i, 0))

    def prev(n):
        return pl.BlockSpec((1, 8, n), lambda bi, i: (bi, jnp.maximum(i * (bt // 8) - 1, 0), 0))

    wspec = pl.BlockSpec((RW_LORA_COLS, width), lambda bi, i: (0, 0))
    ospec = pl.BlockSpec((1, bt, width), lambda bi, i: (bi, i, 0))
    return pl.pallas_call(
        functools.partial(_rw_prep_kernel, width=width),
        out_shape=[jax.ShapeDtypeStruct((b, t, width), F32)] * 8,
        grid=(b, t // bt),
        in_specs=[cur(cols), prev(cols), cur(RW_LORA_COLS), prev(RW_LORA_COLS),
                  vspec(cols), vspec(RW_LORA_COLS),
                  vspec(width), vspec(width), vspec(width), vspec(width), vspec(width),
                  wspec, wspec, wspec],
        out_specs=[ospec] * 8,
        compiler_params=_cparams("parallel", "arbitrary"),
        name="rwkv7_prep",
    )(rkv, rkv, lora, lora, vec(mu[:cols]), vec(mu[cols:]), vec(w0), vec(a0), vec(k_k), vec(k_a),
      vec(r_k), w2w, w2a, w2g)


def _tile_rows(x, n):
    return jnp.concatenate([x] * n, axis=0)


def _rw_rec_kernel(r_ref, lw_ref, k_ref, v_ref, kk_ref, kb_ref, g_ref, bonus_ref,
                   lnw_ref, lnb_ref, o_ref, ht_ref, *, chunk, groups, inv_parts):
    nh, hd, gw = RW_GROUP_HEADS, RW_HEAD_DIM, RW_GROUP
    wide = nh * chunk
    tb = r_ref.shape[1]

    @pl.when(pl.program_id(2) == 0)
    def _():
        ht_ref[...] = jnp.zeros_like(ht_ref)

    def iota(shape, d):
        return lax.broadcasted_iota(jnp.int32, shape, d)

    ltri = jnp.where(iota((chunk, chunk), 1) <= iota((chunk, chunk), 0), 1.0, 0.0).astype(BF16)
    row_cw = iota((chunk, wide), 0)
    pos_cw = iota((chunk, wide), 1) % chunk
    strict = pos_cw < row_cw
    incl = pos_cw <= row_cw
    eye_t = jnp.where(pos_cw == row_cw, 1.0, 0.0)
    off_diag = []
    s = 1
    while s < chunk:
        off_diag.append(((row_cw // (2 * s)) == (pos_cw // (2 * s)))
                        & ((row_cw % (2 * s)) >= s) & ((pos_cw % (2 * s)) < s))
        s *= 2
    bd_wk = (iota((wide, gw), 0) // chunk) == (iota((wide, gw), 1) // hd)
    bd_ww = (iota((wide, wide), 0) // chunk) == (iota((wide, wide), 1) // chunk)
    head_mask = (iota((gw, gw), 0) // hd) == (iota((gw, gw), 1) // hd)
    ones_bd = _head_ones(gw, hd)

    def expand_k(x):
        return jnp.where(bd_wk, _tile_rows(x, nh), 0.0).astype(BF16)

    def expand_w(x):
        return jnp.where(bd_ww, _tile_rows(x, nh), 0.0)

    def mm(a, b):
        if inv_parts == 1:
            return _dot(a.astype(BF16), b.astype(BF16))
        ah, al = _split2(a)
        bh, bl = _split2(b)
        return _dot(ah, bh) + (_dot(ah, bl) + _dot(al, bh))

    def step(c, _):
        rows = pl.ds(pl.multiple_of(c * chunk, chunk), chunk)
        gs = range(groups)
        lanes = [slice(gi * gw, (gi + 1) * gw) for gi in gs]
        lw = lw_ref[0, rows, :]
        gam = _dot_exact_rhs01(ltri, lw, parts=2)
        gam_end = gam[chunk - 1:chunk, :]
        e_neg = jnp.exp(-gam)
        e_suf = jnp.exp(gam_end - gam)
        k_all = k_ref[0, rows, :]
        kb_all = kb_ref[0, rows, :]
        v_all = v_ref[0, rows, :]
        a_t = -kk_ref[0, rows, :] * jnp.exp(gam - lw)
        r_t = r_ref[0, rows, :] * jnp.exp(gam)
        b_t = kb_all * e_neg
        k_t = k_all * e_neg
        b_h = kb_all * e_suf
        k_h = k_all * e_suf
        h_decay = jnp.exp(gam_end)

        lhs = [jnp.concatenate([a_t[:, l], r_t[:, l]], axis=0).astype(BF16) for l in lanes]
        sb = [_dot_nt(lhs[g], expand_k(b_t[:, lanes[g]])) for g in gs]
        sk = [_dot_nt(lhs[g], expand_k(k_t[:, lanes[g]])) for g in gs]
        a_ab = [jnp.where(strict, s[:chunk], 0.0) for s in sb]
        a_rb = [jnp.where(incl, s[chunk:], 0.0) for s in sb]
        a_ak = [jnp.where(strict, s[:chunk], 0.0) for s in sk]
        a_rk = [jnp.where(incl, s[chunk:], 0.0) for s in sk]

        hts = [ht_ref[g] for g in gs]
        from_state = [_dot_nt(lhs[g], hts[g].astype(BF16)) for g in gs]
        from_v = [_dot(jnp.concatenate([a_ak[g], a_rk[g]], axis=0).astype(BF16),
                       expand_k(v_all[:, lanes[g]])) for g in gs]

        p_acc = [eye_t + jnp.where(off_diag[0], a, 0.0) for a in a_ab]
        for li in range(1, len(off_diag)):
            left = [mm(p_acc[g], expand_w(jnp.where(off_diag[li], a_ab[g], 0.0))) for g in gs]
            p_acc = [p_acc[g] + mm(left[g], expand_w(p_acc[g])) for g in gs]

        u = [_dot(p_acc[g].astype(BF16), expand_k(from_state[g][:chunk] + from_v[g][:chunk]))
             for g in gs]
        y = [from_state[g][chunk:] + from_v[g][chunk:] + _dot(a_rb[g].astype(BF16), expand_k(u[g]))
             for g in gs]
        upd = [_dot_tn(jnp.concatenate([u[g], v_all[:, lanes[g]]], axis=0).astype(BF16),
                       jnp.concatenate([b_h[:, lanes[g]], k_h[:, lanes[g]]], axis=0).astype(BF16))
               for g in gs]
        for g in gs:
            ht_ref[g] = hts[g] * h_decay[:, lanes[g]] + jnp.where(head_mask, upd[g], 0.0)

        y_all = jnp.concatenate(y, axis=0)
        yc_all = y_all - _head_sums(y_all, ones_bd) * (1.0 / hd)
        var_all = _head_sums(yc_all * yc_all, ones_bd) * (1.0 / hd)
        yn_all = yc_all * lax.rsqrt(var_all + RW_GN_EPS)
        for g in gs:
            l = lanes[g]
            yn = yn_all[g * chunk:(g + 1) * chunk] * lnw_ref[:, l] + lnb_ref[:, l]
            out = (yn + bonus_ref[0, rows, l]) * g_ref[0, rows, l]
            o_ref[0, rows, l] = out.astype(o_ref.dtype)
        return 0

    lax.fori_loop(0, tb // chunk, step, 0)


def _rw_recurrence(r, lw, k, v, kk, kb, g, bonus, lnx_w, lnx_b, *, chunk=64, groups=8, tb=256,
                   inv_parts=1):
    b, t, width = r.shape
    tb = min(tb, t)
    bw = groups * RW_GROUP
    spec = pl.BlockSpec((1, tb, bw), lambda bi, gi, ti: (bi, ti, gi))
    vspec = pl.BlockSpec((1, bw), lambda bi, gi, ti: (0, gi))
    return pl.pallas_call(
        functools.partial(_rw_rec_kernel, chunk=chunk, groups=groups, inv_parts=inv_parts),
        out_shape=jax.ShapeDtypeStruct((b, t, width), BF16),
        grid=(b, width // bw, t // tb),
        in_specs=[spec] * 8 + [vspec, vspec],
        out_specs=spec,
        scratch_shapes=[pltpu.VMEM((groups, RW_GROUP, RW_GROUP), F32)],
        compiler_params=_cparams("parallel", "parallel", "arbitrary"),
        name="rwkv7_recurrence",
    )(r, lw, k, v, kk, kb, g, bonus,
      lnx_w.reshape(1, width).astype(F32), lnx_b.reshape(1, width).astype(F32))


def rwkv7(rkv, lora, mu, w0, w_w2, a0, w_a2, w_g2, k_k, k_a, r_k, lnx_w, lnx_b, **rec_kw):
    parts = _rw_prep(rkv, lora, mu, w0, w_w2, a0, w_a2, w_g2, k_k, k_a, r_k)
    return _rw_recurrence(*parts, lnx_w, lnx_b, **rec_kw)


def kernel(x, norm1_g, w_in, sb_norm_g, hg_lb_param, hg_norm_g, rw_mu, rw_w0, rw_w_w2, rw_a0,
           rw_w_a2, rw_w_g2, rw_k_k, rw_k_a, rw_r_k, rw_lnx_w, rw_lnx_b, w_out, norm2_g,
           w_ff_in, w_ff_out, final_g):
    b, t, d = x.shape
    depth = norm1_g.shape[0]
    sb_w = sb_norm_g.shape[1]
    hg_w = hg_lb_param.shape[1]
    rw_w = rw_lnx_w.shape[1]
    sb_cols, hg_cols, rkv_cols = 3 * sb_w, 4 * hg_w, 3 * rw_w
    m = b * t
    xf = x.reshape(m, d)
    w_in_b, w_out_b = w_in.astype(BF16), w_out.astype(BF16)
    w_ff_in_b, w_ff_out_b = w_ff_in.astype(BF16), w_ff_out.astype(BF16)
    h, ss = rmsnorm(xf, norm1_g[0], BF16), None
    for l in range(depth):
        proj = functools.partial(matmul, h, w_in_b, layer=l, bm=1024, row_ss=ss)
        p_sb = proj(bn=1024, col0=0, n=sb_cols, out_dtype=BF16, name="proj_sb").reshape(b, t, -1)
        p_hg = proj(bn=1024, col0=sb_cols, n=hg_cols, name="proj_hg").reshape(b, t, -1)
        p_rkv = proj(bn=1024, col0=sb_cols + hg_cols, n=rkv_cols, name="proj_rkv").reshape(b, t, -1)
        p_lora = proj(bn=RW_LORA_COLS, col0=sb_cols + hg_cols + rkv_cols, n=RW_LORA_COLS,
                      name="proj_lora").reshape(b, t, -1)
        o_sb = sb_attention(p_sb, sb_norm_g[l])
        o_hg = hgrn2(p_hg, hg_lb_param, hg_norm_g[l], layer=l)
        o_rw = rwkv7(p_rkv, p_lora, rw_mu[l], rw_w0[l], rw_w_w2[l], rw_a0[l], rw_w_a2[l],
                     rw_w_g2[l], rw_k_k[l], rw_k_a[l], rw_r_k[l], rw_lnx_w[l], rw_lnx_b[l])
        mix = [o.reshape(m, -1) for o in (o_sb, o_hg, o_rw)]
        xf, h, ss = matmul(mix, w_out_b, layer=l, bm=1024, bn=512, residual=xf,
                           next_norm_g=norm2_g[l], name="out_proj")
        ff = matmul(h, w_ff_in_b, layer=l, bm=1024, bn=1024, relu2=True, row_ss=ss, out_dtype=BF16,
                    name="ff_in")
        ff_out = functools.partial(matmul, ff, w_ff_out_b, layer=l, bm=1024, bn=1024, bk=2048,
                                   residual=xf)
        if l + 1 < depth:
            xf, h, ss = ff_out(next_norm_g=norm1_g[l + 1], name="ff_out_norm")
        else:
            xf = ff_out(name="ff_out")
    return rmsnorm(xf, final_g, F32).reshape(b, t, d)
```

```python
import functools
import math

import jax
import jax.numpy as jnp
import numpy as np
from jax import lax
from jax.experimental import pallas as pl
from jax.experimental.pallas import tpu as pltpu

F32 = jnp.float32
BF16 = jnp.bfloat16

NORM_EPS = 1e-5
V7X_VMEM_BYTES = 64 * 1024 * 1024
VMEM_LIMIT = 56 * 1024 * 1024
LANES = 128

SB_HEAD_DIM = 128
SB_LOG_WEIGHT_CUTOFF = -90.0
HG_HEAD_DIM = 128
HG_MIN_F = 1e-30
RW_HEAD_DIM = 64
RW_GN_EPS = 64e-5
RW_DECAY_RANK = 96
RW_AAA_RANK = 96
RW_GATE_RANK = 64
RW_LORA_COLS = RW_DECAY_RANK + RW_AAA_RANK + RW_GATE_RANK


def _cparams(*sem):
    return pltpu.CompilerParams(dimension_semantics=sem, vmem_limit_bytes=VMEM_LIMIT)


def _dot(a, b):
    return jnp.dot(a, b, preferred_element_type=F32)


def _dot_nt(a, b):
    return lax.dot_general(a, b, (((1,), (1,)), ((), ())), preferred_element_type=F32)


def _dot_tn(a, b):
    return lax.dot_general(a, b, (((0,), (0,)), ((), ())), preferred_element_type=F32)


def _split2(x):
    hi = x.astype(BF16)
    lo = (x - hi.astype(F32)).astype(BF16)
    return hi, lo


def _split3(x):
    hi = x.astype(BF16)
    r1 = x - hi.astype(F32)
    mid = r1.astype(BF16)
    lo = (r1 - mid.astype(F32)).astype(BF16)
    return hi, mid, lo


def _dot_exact_rhs01(a01, x, parts=3):
    ps = _split3(x) if parts == 3 else _split2(x)
    out = _dot(a01, ps[0])
    for p in ps[1:]:
        out = out + _dot(a01, p)
    return out


def _dot_exact_lhs(x, b01, parts=2):
    ps = _split3(x) if parts == 3 else _split2(x)
    out = _dot(ps[0], b01)
    for p in ps[1:]:
        out = out + _dot(p, b01)
    return out


def _rmsnorm_kernel(x_ref, g_ref, o_ref):
    x = x_ref[...]
    ms = jnp.mean(x * x, axis=-1, keepdims=True)
    o_ref[...] = (x * lax.rsqrt(ms + NORM_EPS) * g_ref[...]).astype(o_ref.dtype)


def rmsnorm(x, g, out_dtype, bm=512):
    m, d = x.shape
    bm = min(bm, m)
    return pl.pallas_call(
        _rmsnorm_kernel,
        out_shape=jax.ShapeDtypeStruct((m, d), out_dtype),
        grid=(m // bm,),
        in_specs=[pl.BlockSpec((bm, d), lambda i: (i, 0)),
                  pl.BlockSpec((1, d), lambda i: (0, 0))],
        out_specs=pl.BlockSpec((bm, d), lambda i: (i, 0)),
        compiler_params=_cparams("parallel"),
        name="rmsnorm",
    )(x, g.reshape(1, d).astype(F32))


def _matmul_kernel(*refs, n_lhs, nk, relu2, has_res, has_row_ss, emit_norm, norm_dim):
    refs = list(refs)
    a_refs = [refs.pop(0) for _ in range(n_lhs)]
    b_ref = refs.pop(0)
    r_ref = refs.pop(0) if has_res else None
    ss_in_ref = refs.pop(0) if has_row_ss else None
    gnext_ref = refs.pop(0) if emit_norm else None
    o_ref = refs.pop(0)
    xg_ref, ss_out_ref = (refs.pop(0), refs.pop(0)) if emit_norm else (None, None)
    rest = refs
    j = pl.program_id(1)

    def product():
        acc, k0 = None, 0
        for a_ref in a_refs:
            kw = a_ref.shape[1]
            part = _dot(a_ref[...], b_ref[k0:k0 + kw, :])
            acc = part if acc is None else acc + part
            k0 += kw
        return acc

    def finish(acc):
        if has_row_ss:
            acc = acc * lax.rsqrt(ss_in_ref[:, :1] * (1.0 / norm_dim) + NORM_EPS)
        if relu2:
            acc = jnp.square(jnp.maximum(acc, 0.0))
        if has_res:
            acc = acc + r_ref[...]
        o_ref[...] = acc.astype(o_ref.dtype)
        if emit_norm:
            xg_ref[...] = (acc * gnext_ref[...]).astype(xg_ref.dtype)
            part = jnp.broadcast_to(jnp.sum(acc * acc, axis=-1, keepdims=True), ss_out_ref.shape)

            @pl.when(j == 0)
            def _():
                ss_out_ref[...] = part

            @pl.when(j > 0)
            def _():
                ss_out_ref[...] += part

    if nk == 1:
        finish(product())
        return

    acc_ref = rest[0]
    k = pl.program_id(2)

    @pl.when(k == 0)
    def _():
        acc_ref[...] = product()

    @pl.when((k > 0) & (k < nk - 1))
    def _():
        acc_ref[...] += product()

    @pl.when(k == nk - 1)
    def _():
        finish(acc_ref[...] + product())


def matmul(a, b, *, bm, bn, bk=None, layer=0, col0=0, n=None, relu2=False, residual=None,
           row_ss=None, next_norm_g=None, out_dtype=F32, name="matmul"):
    a_list = list(a) if isinstance(a, (list, tuple)) else [a]
    m = a_list[0].shape[0]
    kdim = sum(x.shape[1] for x in a_list)
    n = b.shape[2] - col0 if n is None else n
    bm, bn = min(bm, m), min(bn, n)
    bk = kdim if bk is None else min(bk, kdim)
    assert b.shape[1] == kdim and m % bm == 0 and n % bn == 0 and kdim % bk == 0 and col0 % bn == 0
    nk = kdim // bk
    assert nk == 1 or len(a_list) == 1
    has_res = residual is not None
    jb0 = col0 // bn
    if len(a_list) == 1:
        in_specs = [pl.BlockSpec((bm, bk), lambda i, j, k: (i, k))]
    else:
        in_specs = [pl.BlockSpec((bm, x.shape[1]), lambda i, j, k: (i, 0)) for x in a_list]
    in_specs.append(pl.BlockSpec((None, bk, bn), lambda i, j, k: (layer, k, jb0 + j)))
    args = a_list + [b]
    if has_res:
        in_specs.append(pl.BlockSpec((bm, bn), lambda i, j, k: (i, j)))
        args.append(residual)
    stat_spec = pl.BlockSpec((bm, LANES), lambda i, j, k: (i, 0))
    if row_ss is not None:
        in_specs.append(stat_spec)
        args.append(row_ss)
    emit_norm = next_norm_g is not None
    out_shape = jax.ShapeDtypeStruct((m, n), out_dtype)
    out_specs = pl.BlockSpec((bm, bn), lambda i, j, k: (i, j))
    if emit_norm:
        assert col0 == 0 and n == b.shape[2]
        in_specs.append(pl.BlockSpec((1, bn), lambda i, j, k: (0, j)))
        args.append(next_norm_g.reshape(1, n).astype(F32))
        out_shape = [out_shape, jax.ShapeDtypeStruct((m, n), BF16),
                     jax.ShapeDtypeStruct((m, LANES), F32)]
        out_specs = [out_specs, pl.BlockSpec((bm, bn), lambda i, j, k: (i, j)), stat_spec]
    return pl.pallas_call(
        functools.partial(_matmul_kernel, n_lhs=len(a_list), nk=nk, relu2=relu2, has_res=has_res,
                          has_row_ss=row_ss is not None, emit_norm=emit_norm, norm_dim=kdim),
        out_shape=out_shape,
        grid=(m // bm, n // bn, nk),
        in_specs=in_specs,
        out_specs=out_specs,
        scratch_shapes=[pltpu.VMEM((bm, bn), F32)] if nk > 1 else [],
        compiler_params=_cparams("parallel", "arbitrary" if emit_norm else "parallel", "arbitrary"),
        name=name,
    )(*args)


def _sb_kernel(q_ref, k_ref, v_ref, g_ref, o_ref, *, blk, heads):
    i = pl.program_id(2)
    scale = SB_HEAD_DIM ** -0.5
    hd = SB_HEAD_DIM
    qs = [q_ref[0, :, h * hd:(h + 1) * hd].astype(BF16) for h in range(heads)]
    row = lax.broadcasted_iota(jnp.int32, (blk, blk), 0)
    col = lax.broadcasted_iota(jnp.int32, (blk, blk), 1)
    before = col < row
    r2 = lax.broadcasted_iota(jnp.int32, (blk, 2 * blk), 0)
    c2 = lax.broadcasted_iota(jnp.int32, (blk, 2 * blk), 1)
    later = jnp.where((r2 > c2) | (c2 >= blk), 1.0, 0.0).astype(BF16)

    def block(j, carry, masked):
        start = pl.multiple_of(j * blk, blk)
        hs = range(heads)
        kbs = [k_ref[0, pl.ds(start, blk), h * hd:(h + 1) * hd].astype(BF16) for h in hs]
        vbs = [v_ref[0, pl.ds(start, blk), h * hd:(h + 1) * hd].astype(BF16) for h in hs]
        zs = [_dot_nt(qs[h], kbs[h]) * scale for h in hs]
        log_betas = [jnp.minimum(z, 0.0) - jnp.log(1.0 + jnp.exp(-jnp.abs(z))) for z in zs]
        log_keeps = [lb - z for lb, z in zip(log_betas, zs)]
        if masked:
            log_keeps = [jnp.where(before, lk, 0.0) for lk in log_keeps]
        cs_all = _dot_exact_lhs(jnp.concatenate(log_keeps, axis=0), later, parts=2)
        css = [cs_all[h * blk:(h + 1) * blk] for h in hs]
        ws = [jnp.exp(log_betas[h] + carry[2 * h + 1] + css[h][:, :blk]) for h in hs]
        if masked:
            ws = [jnp.where(before, w, 0.0) for w in ws]
        out = []
        for h in hs:
            out += [carry[2 * h] + _dot(ws[h].astype(BF16), vbs[h]), carry[2 * h + 1] + css[h][:, blk:]]
        return tuple(out)

    carry = (jnp.zeros((blk, hd), F32), jnp.zeros((blk, blk), F32)) * heads
    carry = block(i, carry, True)

    def alive(state):
        j = state[0]
        top = state[2]
        for h in range(1, heads):
            top = jnp.maximum(top, state[2 + 2 * h])
        return (j >= 0) & (jnp.max(top) > SB_LOG_WEIGHT_CUTOFF)

    def body(state):
        j = state[0]
        return (j - 1,) + block(j, state[1:], False)

    state = lax.while_loop(alive, body, (i - 1,) + carry)
    for h in range(heads):
        acc = state[1 + 2 * h]
        ms = jnp.mean(acc * acc, axis=-1, keepdims=True)
        o_ref[0, :, h * hd:(h + 1) * hd] = (
            acc * lax.rsqrt(ms + NORM_EPS) * g_ref[:, h * hd:(h + 1) * hd]).astype(o_ref.dtype)


def sb_attention(proj_sb, norm_g, *, blk=128, heads=8):
    b, t, w3 = proj_sb.shape
    w = w3 // 3
    bw = heads * SB_HEAD_DIM
    nh = w // bw
    blk = min(blk, t)
    return pl.pallas_call(
        functools.partial(_sb_kernel, blk=blk, heads=heads),
        out_shape=jax.ShapeDtypeStruct((b, t, w), BF16),
        grid=(b, nh, t // blk),
        in_specs=[pl.BlockSpec((1, blk, bw), lambda bi, h, i: (bi, i, h)),
                  pl.BlockSpec((1, t, bw), lambda bi, h, i: (bi, 0, nh + h)),
                  pl.BlockSpec((1, t, bw), lambda bi, h, i: (bi, 0, 2 * nh + h)),
                  pl.BlockSpec((1, bw), lambda bi, h, i: (0, h))],
        out_specs=pl.BlockSpec((1, blk, bw), lambda bi, h, i: (bi, i, h)),
        compiler_params=_cparams("parallel", "parallel", "arbitrary"),
        name="sb_attention",
    )(proj_sb, proj_sb, proj_sb, norm_g.reshape(1, w).astype(F32))


def _hg_level_halves(chunk):
    halves = []
    m = chunk // 2
    while m >= 1:
        halves.append(m)
        m //= 2
    return halves


HG_ROW_TILE = 8


def _hg_decay_selectors(chunk):
    t = lax.broadcasted_iota(jnp.int32, (chunk, chunk), 0)
    j = lax.broadcasted_iota(jnp.int32, (chunk, chunk), 1)
    mats = [j <= t]
    for m in _hg_level_halves(chunk):
        if m >= HG_ROW_TILE:
            continue
        mid = (t // (2 * m)) * (2 * m) + m - 1
        upper = (t % (2 * m)) >= m
        mats.append((upper & (j > mid) & (j <= t)) | (~upper & (j > t) & (j <= mid)))
    return jnp.concatenate([jnp.where(mm, 1.0, 0.0).astype(BF16) for mm in mats], axis=0)


def _hg_level_decays(sums, chunk):
    b = sums[0:chunk]
    out, small = [], 0
    for m in _hg_level_halves(chunk):
        if m >= HG_ROW_TILE:
            mids = [jnp.broadcast_to(b[s + m - 1:s + m, :], (2 * m, b.shape[1]))
                    for s in range(0, chunk, 2 * m)]
            b_mid = mids[0] if len(mids) == 1 else jnp.concatenate(mids, axis=0)
            out.append(jnp.exp(-jnp.abs(b - b_mid)))
        else:
            small += 1
            out.append(jnp.exp(sums[small * chunk:(small + 1) * chunk]))
    return out


def _hg_kernel(q_ref, f_ref, i_ref, g_ref, lbp_ref, ng_ref, o_ref, st_ref, *, chunk, layer, heads):
    hd = HG_HEAD_DIM
    tb = q_ref.shape[1]

    @pl.when(pl.program_id(2) == 0)
    def _():
        st_ref[...] = jnp.zeros_like(st_ref)

    halves = _hg_level_halves(chunk)
    sel = _hg_decay_selectors(chunk)

    p = lbp_ref[...]
    e = jnp.exp(p - jnp.max(p, axis=0, keepdims=True))
    probs = e / jnp.sum(e, axis=0, keepdims=True)
    lb = jnp.zeros((1, heads * hd), F32)
    for l in range(1, layer + 1):
        lb = lb + probs[l:l + 1, :]

    trow = lax.broadcasted_iota(jnp.int32, (chunk, heads * hd), 0)
    ts = lax.broadcasted_iota(jnp.int32, (chunk, chunk), 0)
    ss = lax.broadcasted_iota(jnp.int32, (chunk, chunk), 1)
    ng = ng_ref[...]

    def step(c, _):
        start = pl.multiple_of(c * chunk, chunk)
        rows = pl.ds(start, chunk)
        q2 = q_ref[0, rows, :]
        fr = f_ref[0, rows, :]
        v2 = i_ref[0, rows, :]
        gate = g_ref[0, rows, :]
        f = lb + (1.0 - lb) * jax.nn.sigmoid(fr)
        log_f = jnp.log(jnp.maximum(f, HG_MIN_F))
        key2 = (1.0 - lb) * jax.nn.sigmoid(-fr)
        sums = _dot_exact_rhs01(sel, log_f, parts=2)
        b = sums[0:chunk]
        b_last = b[chunk - 1:chunk, :]
        level_decay = _hg_level_decays(sums, chunk)
        silu = gate * jax.nn.sigmoid(gate)

        hs = range(heads)
        lanes = [slice(h * hd, (h + 1) * hd) for h in hs]
        states = [st_ref[h] for h in hs]
        vbs = [v2[:, l].astype(BF16) for l in lanes]
        scores = [None] * heads
        for li, m in enumerate(halves):
            el = level_decay[li]
            upper = (trow % (2 * m)) >= m
            ql = jnp.where(upper, q2 * el, 0.0).astype(BF16)
            kl = jnp.where(upper, 0.0, key2 * el).astype(BF16)
            for h in hs:
                sl = _dot_nt(ql[:, lanes[h]], kl[:, lanes[h]])
                if 2 * m < chunk:
                    sl = jnp.where((ts // (2 * m)) == (ss // (2 * m)), sl, 0.0)
                scores[h] = sl if scores[h] is None else scores[h] + sl
        q_pre = (q2 * jnp.exp(b)).astype(BF16)
        k_suf = (key2 * jnp.exp(b_last - b)).astype(BF16)
        qk = q2 * key2
        inter = [_dot_nt(q_pre[:, lanes[h]], states[h].astype(BF16)) for h in hs]
        intra = [_dot(scores[h].astype(BF16), vbs[h]) for h in hs]
        upd = [_dot_tn(vbs[h], k_suf[:, lanes[h]]) for h in hs]
        decay_all = jnp.exp(b_last)
        for h in hs:
            l = lanes[h]
            st_ref[h] = states[h] * decay_all[:, l] + upd[h]
            diag = jnp.sum(qk[:, l], axis=-1, keepdims=True)
            o = intra[h] + diag * v2[:, l] + inter[h]
            ms = jnp.mean(o * o, axis=-1, keepdims=True)
            o = o * lax.rsqrt(ms + NORM_EPS) * ng
            o_ref[0, rows, l] = (o * silu[:, l]).astype(o_ref.dtype)
        return 0

    lax.fori_loop(0, tb // chunk, step, 0)


def hgrn2(proj_hg, lb_param, norm_g, *, layer, chunk=128, heads=4, tb=512):
    b, t, w4 = proj_hg.shape
    w = w4 // 4
    bw = heads * HG_HEAD_DIM
    nh = w // bw
    tb = min(tb, t)
    chunk = min(chunk, tb)
    depth = lb_param.shape[0]

    def col(g):
        return pl.BlockSpec((1, tb, bw), lambda bi, h, ti: (bi, ti, g * nh + h))

    return pl.pallas_call(
        functools.partial(_hg_kernel, chunk=chunk, layer=layer, heads=heads),
        out_shape=jax.ShapeDtypeStruct((b, t, w), BF16),
        grid=(b, nh, t // tb),
        in_specs=[col(0), col(1), col(2), col(3),
                  pl.BlockSpec((depth, bw), lambda bi, h, ti: (0, h)),
                  pl.BlockSpec((1, HG_HEAD_DIM), lambda bi, h, ti: (0, 0))],
        out_specs=pl.BlockSpec((1, tb, bw), lambda bi, h, ti: (bi, ti, h)),
        scratch_shapes=[pltpu.VMEM((heads, HG_HEAD_DIM, HG_HEAD_DIM), F32)],
        compiler_params=_cparams("parallel", "parallel", "arbitrary"),
        name="hgrn2",
    )(proj_hg, proj_hg, proj_hg, proj_hg, lb_param.astype(F32),
      norm_g.reshape(1, HG_HEAD_DIM).astype(F32))


RW_GROUP_HEADS = 4
RW_GROUP = RW_GROUP_HEADS * RW_HEAD_DIM


def _head_ones(n, head):
    r = lax.broadcasted_iota(jnp.int32, (n, n), 0)
    c = lax.broadcasted_iota(jnp.int32, (n, n), 1)
    return jnp.where((r // head) == (c // head), 1.0, 0.0).astype(BF16)


def _head_sums(x, ones_bd):
    n = ones_bd.shape[0]
    outs = [_dot_exact_lhs(x[:, s:s + n], ones_bd, parts=2) for s in range(0, x.shape[1], n)]
    return outs[0] if len(outs) == 1 else jnp.concatenate(outs, axis=1)


def _softplus(y):
    return jnp.maximum(y, 0.0) + jnp.log1p(jnp.exp(-jnp.abs(y)))


def _rw_prep_kernel(x_ref, xp_ref, lo_ref, lop_ref, mu_ref, mulo_ref, w0_ref, a0_ref, kk_ref, ka_ref,
                    rk_ref, w2w_ref, w2a_ref, w2g_ref,
                    r_out, lw_out, k_out, v_out, kk_out, kb_out, g_out, bonus_out, *, width):
    i = pl.program_id(1)

    def token_shift(cur_ref, prev_ref, mix_ref):
        x = cur_ref[0]
        prev_row = jnp.where(i == 0, 0.0, prev_ref[0, 7:8, :])
        row = lax.broadcasted_iota(jnp.int32, x.shape, 0)
        prev = jnp.where(row == 0, prev_row, pltpu.roll(x, 1, axis=0))
        return x + mix_ref[...] * (prev - x)

    xs = token_shift(x_ref, xp_ref, mu_ref)
    lora = token_shift(lo_ref, lop_ref, mulo_ref)
    w_log = -_softplus(-(w0_ref[...] + _dot(jnp.tanh(lora).astype(BF16), w2w_ref[...]))) - 0.5
    log_decay = -jnp.exp(w_log)
    a = jax.nn.sigmoid(a0_ref[...] + _dot(lora.astype(BF16), w2a_ref[...]))
    g = _dot(jax.nn.sigmoid(lora).astype(BF16), w2g_ref[...])

    r = xs[:, :width]
    k = xs[:, width:2 * width]
    v = xs[:, 2 * width:3 * width]
    ones_bd = _head_ones(RW_GROUP, RW_HEAD_DIM)
    kk = k * kk_ref[...]
    kk = kk * lax.rsqrt(jnp.maximum(_head_sums(kk * kk, ones_bd), 1e-24))
    k = k * (1.0 + (a - 1.0) * ka_ref[...])
    bonus = _head_sums(r * k * rk_ref[...], ones_bd) * v

    r_out[0] = r
    lw_out[0] = log_decay
    k_out[0] = k
    v_out[0] = v
    kk_out[0] = kk
    kb_out[0] = kk * a
    g_out[0] = g
    bonus_out[0] = bonus


def _rw_prep(rkv, lora, mu, w0, w_w2, a0, w_a2, w_g2, k_k, k_a, r_k, *, bt=128):
    b, t, cols = rkv.shape
    width = cols // 3
    bt = min(bt, t)

    def padded(w2, first_row):
        full = jnp.zeros((RW_LORA_COLS, width), F32)
        return lax.dynamic_update_slice(full, w2.astype(F32), (first_row, 0)).astype(BF16)

    w2w = padded(w_w2, 0)
    w2a = padded(w_a2, RW_DECAY_RANK)
    w2g = padded(w_g2, RW_DECAY_RANK + RW_AAA_RANK)

    def vec(p):
        return p.reshape(1, -1).astype(F32)

    def vspec(n):
        return pl.BlockSpec((1, n), lambda bi, i: (0, 0))

    def cur(n):
        return pl.BlockSpec((1, bt, n), lambda bi, i: (bi, i, 0))

    def prev(n):
        return pl.BlockSpec((1, 8, n), lambda bi, i: (bi, jnp.maximum(i * (bt // 8) - 1, 0), 0))

    wspec = pl.BlockSpec((RW_LORA_COLS, width), lambda bi, i: (0, 0))
    ospec = pl.BlockSpec((1, bt, width), lambda bi, i: (bi, i, 0))
    return pl.pallas_call(
        functools.partial(_rw_prep_kernel, width=width),
        out_shape=[jax.ShapeDtypeStruct((b, t, width), F32)] * 8,
        grid=(b, t // bt),
        in_specs=[cur(cols), prev(cols), cur(RW_LORA_COLS), prev(RW_LORA_COLS),
                  vspec(cols), vspec(RW_LORA_COLS),
                  vspec(width), vspec(width), vspec(width), vspec(width), vspec(width),
                  wspec, wspec, wspec],
        out_specs=[ospec] * 8,
        compiler_params=_cparams("parallel", "arbitrary"),
        name="rwkv7_prep",
    )(rkv, rkv, lora, lora, vec(mu[:cols]), vec(mu[cols:]), vec(w0), vec(a0), vec(k_k), vec(k_a),
      vec(r_k), w2w, w2a, w2g)


def _tile_rows(x, n):
    return jnp.concatenate([x] * n, axis=0)


def _rw_rec_kernel(r_ref, lw_ref, k_ref, v_ref, kk_ref, kb_ref, g_ref, bonus_ref,
                   lnw_ref, lnb_ref, o_ref, ht_ref, *, chunk, groups, inv_parts):
    nh, hd, gw = RW_GROUP_HEADS, RW_HEAD_DIM, RW_GROUP
    wide = nh * chunk
    tb = r_ref.shape[1]

    @pl.when(pl.program_id(2) == 0)
    def _():
        ht_ref[...] = jnp.zeros_like(ht_ref)

    def iota(shape, d):
        return lax.broadcasted_iota(jnp.int32, shape, d)

    ltri = jnp.where(iota((chunk, chunk), 1) <= iota((chunk, chunk), 0), 1.0, 0.0).astype(BF16)
    row_cw = iota((chunk, wide), 0)
    pos_cw = iota((chunk, wide), 1) % chunk
    strict = pos_cw < row_cw
    incl = pos_cw <= row_cw
    eye_t = jnp.where(pos_cw == row_cw, 1.0, 0.0)
    off_diag = []
    s = 1
    while s < chunk:
        off_diag.append(((row_cw // (2 * s)) == (pos_cw // (2 * s)))
                        & ((row_cw % (2 * s)) >= s) & ((pos_cw % (2 * s)) < s))
        s *= 2
    bd_wk = (iota((wide, gw), 0) // chunk) == (iota((wide, gw), 1) // hd)
    bd_ww = (iota((wide, wide), 0) // chunk) == (iota((wide, wide), 1) // chunk)
    head_mask = (iota((gw, gw), 0) // hd) == (iota((gw, gw), 1) // hd)
    ones_bd = _head_ones(gw, hd)

    def expand_k(x):
        return jnp.where(bd_wk, _tile_rows(x, nh), 0.0).astype(BF16)

    def expand_w(x):
        return jnp.where(bd_ww, _tile_rows(x, nh), 0.0)

    def mm(a, b):
        if inv_parts == 1:
            return _dot(a.astype(BF16), b.astype(BF16))
        ah, al = _split2(a)
        bh, bl = _split2(b)
        return _dot(ah, bh) + (_dot(ah, bl) + _dot(al, bh))

    def step(c, _):
        rows = pl.ds(pl.multiple_of(c * chunk, chunk), chunk)
        gs = range(groups)
        lanes = [slice(gi * gw, (gi + 1) * gw) for gi in gs]
        lw = lw_ref[0, rows, :]
        gam = _dot_exact_rhs01(ltri, lw, parts=2)
        gam_end = gam[chunk - 1:chunk, :]
        e_neg = jnp.exp(-gam)
        e_suf = jnp.exp(gam_end - gam)
        k_all = k_ref[0, rows, :]
        kb_all = kb_ref[0, rows, :]
        v_all = v_ref[0, rows, :]
        a_t = -kk_ref[0, rows, :] * jnp.exp(gam - lw)
        r_t = r_ref[0, rows, :] * jnp.exp(gam)
        b_t = kb_all * e_neg
        k_t = k_all * e_neg
        b_h = kb_all * e_suf
        k_h = k_all * e_suf
        h_decay = jnp.exp(gam_end)

        lhs = [jnp.concatenate([a_t[:, l], r_t[:, l]], axis=0).astype(BF16) for l in lanes]
        sb = [_dot_nt(lhs[g], expand_k(b_t[:, lanes[g]])) for g in gs]
        sk = [_dot_nt(lhs[g], expand_k(k_t[:, lanes[g]])) for g in gs]
        a_ab = [jnp.where(strict, s[:chunk], 0.0) for s in sb]
        a_rb = [jnp.where(incl, s[chunk:], 0.0) for s in sb]
        a_ak = [jnp.where(strict, s[:chunk], 0.0) for s in sk]
        a_rk = [jnp.where(incl, s[chunk:], 0.0) for s in sk]

        hts = [ht_ref[g] for g in gs]
        from_state = [_dot_nt(lhs[g], hts[g].astype(BF16)) for g in gs]
        from_v = [_dot(jnp.concatenate([a_ak[g], a_rk[g]], axis=0).astype(BF16),
                       expand_k(v_all[:, lanes[g]])) for g in gs]

        p_acc = [eye_t + jnp.where(off_diag[0], a, 0.0) for a in a_ab]
        for li in range(1, len(off_diag)):
            left = [mm(p_acc[g], expand_w(jnp.where(off_diag[li], a_ab[g], 0.0))) for g in gs]
            p_acc = [p_acc[g] + mm(left[g], expand_w(p_acc[g])) for g in gs]

        u = [_dot(p_acc[g].astype(BF16), expand_k(from_state[g][:chunk] + from_v[g][:chunk]))
             for g in gs]
        y = [from_state[g][chunk:] + from_v[g][chunk:] + _dot(a_rb[g].astype(BF16), expand_k(u[g]))
             for g in gs]
        upd = [_dot_tn(jnp.concatenate([u[g], v_all[:, lanes[g]]], axis=0).astype(BF16),
                       jnp.concatenate([b_h[:, lanes[g]], k_h[:, lanes[g]]], axis=0).astype(BF16))
               for g in gs]
        for g in gs:
            ht_ref[g] = hts[g] * h_decay[:, lanes[g]] + jnp.where(head_mask, upd[g], 0.0)

        y_all = jnp.concatenate(y, axis=0)
        yc_all = y_all - _head_sums(y_all, ones_bd) * (1.0 / hd)
        var_all = _head_sums(yc_all * yc_all, ones_bd) * (1.0 / hd)
        yn_all = yc_all * lax.rsqrt(var_all + RW_GN_EPS)
        for g in gs:
            l = lanes[g]
            yn = yn_all[g * chunk:(g + 1) * chunk] * lnw_ref[:, l] + lnb_ref[:, l]
            out = (yn + bonus_ref[0, rows, l]) * g_ref[0, rows, l]
            o_ref[0, rows, l] = out.astype(o_ref.dtype)
        return 0

    lax.fori_loop(0, tb // chunk, step, 0)


def _rw_recurrence(r, lw, k, v, kk, kb, g, bonus, lnx_w, lnx_b, *, chunk=64, groups=8, tb=256,
                   inv_parts=1):
    b, t, width = r.shape
    tb = min(tb, t)
    bw = groups * RW_GROUP
    spec = pl.BlockSpec((1, tb, bw), lambda bi, gi, ti: (bi, ti, gi))
    vspec = pl.BlockSpec((1, bw), lambda bi, gi, ti: (0, gi))
    return pl.pallas_call(
        functools.partial(_rw_rec_kernel, chunk=chunk, groups=groups, inv_parts=inv_parts),
        out_shape=jax.ShapeDtypeStruct((b, t, width), BF16),
        grid=(b, width // bw, t // tb),
        in_specs=[spec] * 8 + [vspec, vspec],
        out_specs=spec,
        scratch_shapes=[pltpu.VMEM((groups, RW_GROUP, RW_GROUP), F32)],
        compiler_params=_cparams("parallel", "parallel", "arbitrary"),
        name="rwkv7_recurrence",
    )(r, lw, k, v, kk, kb, g, bonus,
      lnx_w.reshape(1, width).astype(F32), lnx_b.reshape(1, width).astype(F32))


def rwkv7(rkv, lora, mu, w0, w_w2, a0, w_a2, w_g2, k_k, k_a, r_k, lnx_w, lnx_b, **rec_kw):
    parts = _rw_prep(rkv, lora, mu, w0, w_w2, a0, w_a2, w_g2, k_k, k_a, r_k)
    return _rw_recurrence(*parts, lnx_w, lnx_b, **rec_kw)


def kernel(x, norm1_g, w_in, sb_norm_g, hg_lb_param, hg_norm_g, rw_mu, rw_w0, rw_w_w2, rw_a0,
           rw_w_a2, rw_w_g2, rw_k_k, rw_k_a, rw_r_k, rw_lnx_w, rw_lnx_b, w_out, norm2_g,
           w_ff_in, w_ff_out, final_g):
    b, t, d = x.shape
    depth = norm1_g.shape[0]
    sb_w = sb_norm_g.shape[1]
    hg_w = hg_lb_param.shape[1]
    rw_w = rw_lnx_w.shape[1]
    sb_cols, hg_cols, rkv_cols = 3 * sb_w, 4 * hg_w, 3 * rw_w
    m = b * t
    xf = x.reshape(m, d)
    w_in_b, w_out_b = w_in.astype(BF16), w_out.astype(BF16)
    w_ff_in_b, w_ff_out_b = w_ff_in.astype(BF16), w_ff_out.astype(BF16)
    h, ss = rmsnorm(xf, norm1_g[0], BF16), None
    for l in range(depth):
        proj = functools.partial(matmul, h, w_in_b, layer=l, bm=1024, row_ss=ss)
        p_sb = proj(bn=1024, col0=0, n=sb_cols, out_dtype=BF16, name="proj_sb").reshape(b, t, -1)
        p_hg = proj(bn=1024, col0=sb_cols, n=hg_cols, name="proj_hg").reshape(b, t, -1)
        p_rkv = proj(bn=1024, col0=sb_cols + hg_cols, n=rkv_cols, name="proj_rkv").reshape(b, t, -1)
        p_lora = proj(bn=RW_LORA_COLS, col0=sb_cols + hg_cols + rkv_cols, n=RW_LORA_COLS,
                      name="proj_lora").reshape(b, t, -1)
        o_sb = sb_attention(p_sb, sb_norm_g[l])
        o_hg = hgrn2(p_hg, hg_lb_param, hg_norm_g[l], layer=l)
        o_rw = rwkv7(p_rkv, p_lora, rw_mu[l], rw_w0[l], rw_w_w2[l], rw_a0[l], rw_w_a2[l],
                     rw_w_g2[l], rw_k_k[l], rw_k_a[l], rw_r_k[l], rw_lnx_w[l], rw_lnx_b[l])
        mix = [o.reshape(m, -1) for o in (o_sb, o_hg, o_rw)]
        xf, h, ss = matmul(mix, w_out_b, layer=l, bm=1024, bn=512, residual=xf,
                           next_norm_g=norm2_g[l], name="out_proj")
        ff = matmul(h, w_ff_in_b, layer=l, bm=1024, bn=1024, relu2=True, row_ss=ss, out_dtype=BF16,
                    name="ff_in")
        ff_out = functools.partial(matmul, ff, w_ff_out_b, layer=l, bm=1024, bn=1024, bk=2048,
                                   residual=xf)
        if l + 1 < depth:
            xf, h, ss = ff_out(next_norm_g=norm1_g[l + 1], name="ff_out_norm")
        else:
            xf = ff_out(name="ff_out")
    return rmsnorm(xf, final_g, F32).reshape(b, t, d)
```

```python
import functools
import math

import jax
import jax.numpy as jnp
import numpy as np
from jax import lax
from jax.experimental import pallas as pl
from jax.experimental.pallas import tpu as pltpu

F32 = jnp.float32
BF16 = jnp.bfloat16

NORM_EPS = 1e-5
V7X_VMEM_BYTES = 64 * 1024 * 1024
VMEM_LIMIT = 56 * 1024 * 1024
LANES = 128

SB_HEAD_DIM = 128
SB_LOG_WEIGHT_CUTOFF = -90.0
HG_HEAD_DIM = 128
HG_MIN_F = 1e-30
RW_HEAD_DIM = 64
RW_GN_EPS = 64e-5
RW_DECAY_RANK = 96
RW_AAA_RANK = 96
RW_GATE_RANK = 64
RW_LORA_COLS = RW_DECAY_RANK + RW_AAA_RANK + RW_GATE_RANK


def _cparams(*sem):
    return pltpu.CompilerParams(dimension_semantics=sem, vmem_limit_bytes=VMEM_LIMIT)


def _dot(a, b):
    return jnp.dot(a, b, preferred_element_type=F32)


def _dot_nt(a, b):
    return lax.dot_general(a, b, (((1,), (1,)), ((), ())), preferred_element_type=F32)


def _dot_tn(a, b):
    return lax.dot_general(a, b, (((0,), (0,)), ((), ())), preferred_element_type=F32)


def _split2(x):
    hi = x.astype(BF16)
    lo = (x - hi.astype(F32)).astype(BF16)
    return hi, lo


def _split3(x):
    hi = x.astype(BF16)
    r1 = x - hi.astype(F32)
    mid = r1.astype(BF16)
    lo = (r1 - mid.astype(F32)).astype(BF16)
    return hi, mid, lo


def _dot_exact_rhs01(a01, x, parts=3):
    ps = _split3(x) if parts == 3 else _split2(x)
    out = _dot(a01, ps[0])
    for p in ps[1:]:
        out = out + _dot(a01, p)
    return out


def _dot_exact_lhs(x, b01, parts=2):
    ps = _split3(x) if parts == 3 else _split2(x)
    out = _dot(ps[0], b01)
    for p in ps[1:]:
        out = out + _dot(p, b01)
    return out


def _rmsnorm_kernel(x_ref, g_ref, o_ref):
    x = x_ref[...]
    ms = jnp.mean(x * x, axis=-1, keepdims=True)
    o_ref[...] = (x * lax.rsqrt(ms + NORM_EPS) * g_ref[...]).astype(o_ref.dtype)


def rmsnorm(x, g, out_dtype, bm=512):
    m, d = x.shape
    bm = min(bm, m)
    return pl.pallas_call(
        _rmsnorm_kernel,
        out_shape=jax.ShapeDtypeStruct((m, d), out_dtype),
        grid=(m // bm,),
        in_specs=[pl.BlockSpec((bm, d), lambda i: (i, 0)),
                  pl.BlockSpec((1, d), lambda i: (0, 0))],
        out_specs=pl.BlockSpec((bm, d), lambda i: (i, 0)),
        compiler_params=_cparams("parallel"),
        name="rmsnorm",
    )(x, g.reshape(1, d).astype(F32))


def _matmul_kernel(*refs, n_lhs, nk, relu2, has_res, has_row_ss, emit_norm, norm_dim):
    refs = list(refs)
    a_refs = [refs.pop(0) for _ in range(n_lhs)]
    b_ref = refs.pop(0)
    r_ref = refs.pop(0) if has_res else None
    ss_in_ref = refs.pop(0) if has_row_ss else None
    gnext_ref = refs.pop(0) if emit_norm else None
    o_ref = refs.pop(0)
    xg_ref, ss_out_ref = (refs.pop(0), refs.pop(0)) if emit_norm else (None, None)
    rest = refs
    j = pl.program_id(1)

    def product():
        acc, k0 = None, 0
        for a_ref in a_refs:
            kw = a_ref.shape[1]
            part = _dot(a_ref[...], b_ref[k0:k0 + kw, :])
            acc = part if acc is None else acc + part
            k0 += kw
        return acc

    def finish(acc):
        if has_row_ss:
            acc = acc * lax.rsqrt(ss_in_ref[:, :1] * (1.0 / norm_dim) + NORM_EPS)
        if relu2:
            acc = jnp.square(jnp.maximum(acc, 0.0))
        if has_res:
            acc = acc + r_ref[...]
        o_ref[...] = acc.astype(o_ref.dtype)
        if emit_norm:
            xg_ref[...] = (acc * gnext_ref[...]).astype(xg_ref.dtype)
            part = jnp.broadcast_to(jnp.sum(acc * acc, axis=-1, keepdims=True), ss_out_ref.shape)

            @pl.when(j == 0)
            def _():
                ss_out_ref[...] = part

            @pl.when(j > 0)
            def _():
                ss_out_ref[...] += part

    if nk == 1:
        finish(product())
        return

    acc_ref = rest[0]
    k = pl.program_id(2)

    @pl.when(k == 0)
    def _():
        acc_ref[...] = product()

    @pl.when((k > 0) & (k < nk - 1))
    def _():
        acc_ref[...] += product()

    @pl.when(k == nk - 1)
    def _():
        finish(acc_ref[...] + product())


def matmul(a, b, *, bm, bn, bk=None, layer=0, col0=0, n=None, relu2=False, residual=None,
           row_ss=None, next_norm_g=None, out_dtype=F32, name="matmul"):
    a_list = list(a) if isinstance(a, (list, tuple)) else [a]
    m = a_list[0].shape[0]
    kdim = sum(x.shape[1] for x in a_list)
    n = b.shape[2] - col0 if n is None else n
    bm, bn = min(bm, m), min(bn, n)
    bk = kdim if bk is None else min(bk, kdim)
    assert b.shape[1] == kdim and m % bm == 0 and n % bn == 0 and kdim % bk == 0 and col0 % bn == 0
    nk = kdim // bk
    assert nk == 1 or len(a_list) == 1
    has_res = residual is not None
    jb0 = col0 // bn
    if len(a_list) == 1:
        in_specs = [pl.BlockSpec((bm, bk), lambda i, j, k: (i, k))]
    else:
        in_specs = [pl.BlockSpec((bm, x.shape[1]), lambda i, j, k: (i, 0)) for x in a_list]
    in_specs.append(pl.BlockSpec((None, bk, bn), lambda i, j, k: (layer, k, jb0 + j)))
    args = a_list + [b]
    if has_res:
        in_specs.append(pl.BlockSpec((bm, bn), lambda i, j, k: (i, j)))
        args.append(residual)
    stat_spec = pl.BlockSpec((bm, LANES), lambda i, j, k: (i, 0))
    if row_ss is not None:
        in_specs.append(stat_spec)
        args.append(row_ss)
    emit_norm = next_norm_g is not None
    out_shape = jax.ShapeDtypeStruct((m, n), out_dtype)
    out_specs = pl.BlockSpec((bm, bn), lambda i, j, k: (i, j))
    if emit_norm:
        assert col0 == 0 and n == b.shape[2]
        in_specs.append(pl.BlockSpec((1, bn), lambda i, j, k: (0, j)))
        args.append(next_norm_g.reshape(1, n).astype(F32))
        out_shape = [out_shape, jax.ShapeDtypeStruct((m, n), BF16),
                     jax.ShapeDtypeStruct((m, LANES), F32)]
        out_specs = [out_specs, pl.BlockSpec((bm, bn), lambda i, j, k: (i, j)), stat_spec]
    return pl.pallas_call(
        functools.partial(_matmul_kernel, n_lhs=len(a_list), nk=nk, relu2=relu2, has_res=has_res,
                          has_row_ss=row_ss is not None, emit_norm=emit_norm, norm_dim=kdim),
        out_shape=out_shape,
        grid=(m // bm, n // bn, nk),
        in_specs=in_specs,
        out_specs=out_specs,
        scratch_shapes=[pltpu.VMEM((bm, bn), F32)] if nk > 1 else [],
        compiler_params=_cparams("parallel", "arbitrary" if emit_norm else "parallel", "arbitrary"),
        name=name,
    )(*args)


def _sb_kernel(q_ref, k_ref, v_ref, g_ref, o_ref, *, blk, heads):
    i = pl.program_id(2)
    scale = SB_HEAD_DIM ** -0.5
    hd = SB_HEAD_DIM
    qs = [q_ref[0, :, h * hd:(h + 1) * hd].astype(BF16) for h in range(heads)]
    row = lax.broadcasted_iota(jnp.int32, (blk, blk), 0)
    col = lax.broadcasted_iota(jnp.int32, (blk, blk), 1)
    before = col < row
    r2 = lax.broadcasted_iota(jnp.int32, (blk, 2 * blk), 0)
    c2 = lax.broadcasted_iota(jnp.int32, (blk, 2 * blk), 1)
    later = jnp.where((r2 > c2) | (c2 >= blk), 1.0, 0.0).astype(BF16)

    def block(j, carry, masked):
        start = pl.multiple_of(j * blk, blk)
        hs = range(heads)
        kbs = [k_ref[0, pl.ds(start, blk), h * hd:(h + 1) * hd].astype(BF16) for h in hs]
        vbs = [v_ref[0, pl.ds(start, blk), h * hd:(h + 1) * hd].astype(BF16) for h in hs]
        zs = [_dot_nt(qs[h], kbs[h]) * scale for h in hs]
        log_betas = [jnp.minimum(z, 0.0) - jnp.log(1.0 + jnp.exp(-jnp.abs(z))) for z in zs]
        log_keeps = [lb - z for lb, z in zip(log_betas, zs)]
        if masked:
            log_keeps = [jnp.where(before, lk, 0.0) for lk in log_keeps]
        cs_all = _dot_exact_lhs(jnp.concatenate(log_keeps, axis=0), later, parts=2)
        css = [cs_all[h * blk:(h + 1) * blk] for h in hs]
        ws = [jnp.exp(log_betas[h] + carry[2 * h + 1] + css[h][:, :blk]) for h in hs]
        if masked:
            ws = [jnp.where(before, w, 0.0) for w in ws]
        out = []
        for h in hs:
            out += [carry[2 * h] + _dot(ws[h].astype(BF16), vbs[h]), carry[2 * h + 1] + css[h][:, blk:]]
        return tuple(out)

    carry = (jnp.zeros((blk, hd), F32), jnp.zeros((blk, blk), F32)) * heads
    carry = block(i, carry, True)

    def alive(state):
        j = state[0]
        top = state[2]
        for h in range(1, heads):
            top = jnp.maximum(top, state[2 + 2 * h])
        return (j >= 0) & (jnp.max(top) > SB_LOG_WEIGHT_CUTOFF)

    def body(state):
        j = state[0]
        return (j - 1,) + block(j, state[1:], False)

    state = lax.while_loop(alive, body, (i - 1,) + carry)
    for h in range(heads):
        acc = state[1 + 2 * h]
        ms = jnp.mean(acc * acc, axis=-1, keepdims=True)
        o_ref[0, :, h * hd:(h + 1) * hd] = (
            acc * lax.rsqrt(ms + NORM_EPS) * g_ref[:, h * hd:(h + 1) * hd]).astype(o_ref.dtype)


def sb_attention(proj_sb, norm_g, *, blk=128, heads=8):
    b, t, w3 = proj_sb.shape
    w = w3 // 3
    bw = heads * SB_HEAD_DIM
    nh = w // bw
    blk = min(blk, t)
    return pl.pallas_call(
        functools.partial(_sb_kernel, blk=blk, heads=heads),
        out_shape=jax.ShapeDtypeStruct((b, t, w), BF16),
        grid=(b, nh, t // blk),
        in_specs=[pl.BlockSpec((1, blk, bw), lambda bi, h, i: (bi, i, h)),
                  pl.BlockSpec((1, t, bw), lambda bi, h, i: (bi, 0, nh + h)),
                  pl.BlockSpec((1, t, bw), lambda bi, h, i: (bi, 0, 2 * nh + h)),
                  pl.BlockSpec((1, bw), lambda bi, h, i: (0, h))],
        out_specs=pl.BlockSpec((1, blk, bw), lambda bi, h, i: (bi, i, h)),
        compiler_params=_cparams("parallel", "parallel", "arbitrary"),
        name="sb_attention",
    )(proj_sb, proj_sb, proj_sb, norm_g.reshape(1, w).astype(F32))


def _hg_level_halves(chunk):
    halves = []
    m = chunk // 2
    while m >= 1:
        halves.append(m)
        m //= 2
    return halves


HG_ROW_TILE = 8


def _hg_decay_selectors(chunk):
    t = lax.broadcasted_iota(jnp.int32, (chunk, chunk), 0)
    j = lax.broadcasted_iota(jnp.int32, (chunk, chunk), 1)
    mats = [j <= t]
    for m in _hg_level_halves(chunk):
        if m >= HG_ROW_TILE:
            continue
        mid = (t // (2 * m)) * (2 * m) + m - 1
        upper = (t % (2 * m)) >= m
        mats.append((upper & (j > mid) & (j <= t)) | (~upper & (j > t) & (j <= mid)))
    return jnp.concatenate([jnp.where(mm, 1.0, 0.0).astype(BF16) for mm in mats], axis=0)


def _hg_level_decays(sums, chunk):
    b = sums[0:chunk]
    out, small = [], 0
    for m in _hg_level_halves(chunk):
        if m >= HG_ROW_TILE:
            mids = [jnp.broadcast_to(b[s + m - 1:s + m, :], (2 * m, b.shape[1]))
                    for s in range(0, chunk, 2 * m)]
            b_mid = mids[0] if len(mids) == 1 else jnp.concatenate(mids, axis=0)
            out.append(jnp.exp(-jnp.abs(b - b_mid)))
        else:
            small += 1
            out.append(jnp.exp(sums[small * chunk:(small + 1) * chunk]))
    return out


def _hg_kernel(q_ref, f_ref, i_ref, g_ref, lbp_ref, ng_ref, o_ref, st_ref, *, chunk, layer, heads):
    hd = HG_HEAD_DIM
    tb = q_ref.shape[1]

    @pl.when(pl.program_id(2) == 0)
    def _():
        st_ref[...] = jnp.zeros_like(st_ref)

    halves = _hg_level_halves(chunk)
    sel = _hg_decay_selectors(chunk)

    p = lbp_ref[...]
    e = jnp.exp(p - jnp.max(p, axis=0, keepdims=True))
    probs = e / jnp.sum(e, axis=0, keepdims=True)
    lb = jnp.zeros((1, heads * hd), F32)
    for l in range(1, layer + 1):
        lb = lb + probs[l:l + 1, :]

    trow = lax.broadcasted_iota(jnp.int32, (chunk, heads * hd), 0)
    ts = lax.broadcasted_iota(jnp.int32, (chunk, chunk), 0)
    ss = lax.broadcasted_iota(jnp.int32, (chunk, chunk), 1)
    ng = ng_ref[...]

    def step(c, _):
        start = pl.multiple_of(c * chunk, chunk)
        rows = pl.ds(start, chunk)
        q2 = q_ref[0, rows, :]
        fr = f_ref[0, rows, :]
        v2 = i_ref[0, rows, :]
        gate = g_ref[0, rows, :]
        f = lb + (1.0 - lb) * jax.nn.sigmoid(fr)
        log_f = jnp.log(jnp.maximum(f, HG_MIN_F))
        key2 = (1.0 - lb) * jax.nn.sigmoid(-fr)
        sums = _dot_exact_rhs01(sel, log_f, parts=2)
        b = sums[0:chunk]
        b_last = b[chunk - 1:chunk, :]
        level_decay = _hg_level_decays(sums, chunk)
        silu = gate * jax.nn.sigmoid(gate)

        hs = range(heads)
        lanes = [slice(h * hd, (h + 1) * hd) for h in hs]
        states = [st_ref[h] for h in hs]
        vbs = [v2[:, l].astype(BF16) for l in lanes]
        scores = [None] * heads
        for li, m in enumerate(halves):
            el = level_decay[li]
            upper = (trow % (2 * m)) >= m
            ql = jnp.where(upper, q2 * el, 0.0).astype(BF16)
            kl = jnp.where(upper, 0.0, key2 * el).astype(BF16)
            for h in hs:
                sl = _dot_nt(ql[:, lanes[h]], kl[:, lanes[h]])
                if 2 * m < chunk:
                    sl = jnp.where((ts // (2 * m)) == (ss // (2 * m)), sl, 0.0)
                scores[h] = sl if scores[h] is None else scores[h] + sl
        q_pre = (q2 * jnp.exp(b)).astype(BF16)
        k_suf = (key2 * jnp.exp(b_last - b)).astype(BF16)
        qk = q2 * key2
        inter = [_dot_nt(q_pre[:, lanes[h]], states[h].astype(BF16)) for h in hs]
        intra = [_dot(scores[h].astype(BF16), vbs[h]) for h in hs]
        upd = [_dot_tn(vbs[h], k_suf[:, lanes[h]]) for h in hs]
        decay_all = jnp.exp(b_last)
        for h in hs:
            l = lanes[h]
            st_ref[h] = states[h] * decay_all[:, l] + upd[h]
            diag = jnp.sum(qk[:, l], axis=-1, keepdims=True)
            o = intra[h] + diag * v2[:, l] + inter[h]
            ms = jnp.mean(o * o, axis=-1, keepdims=True)
            o = o * lax.rsqrt(ms + NORM_EPS) * ng
            o_ref[0, rows, l] = (o * silu[:, l]).astype(o_ref.dtype)
        return 0

    lax.fori_loop(0, tb // chunk, step, 0, unroll=2)


def hgrn2(proj_hg, lb_param, norm_g, *, layer, chunk=128, heads=4, tb=512):
    b, t, w4 = proj_hg.shape
    w = w4 // 4
    bw = heads * HG_HEAD_DIM
    nh = w // bw
    tb = min(tb, t)
    chunk = min(chunk, tb)
    depth = lb_param.shape[0]

    def col(g):
        return pl.BlockSpec((1, tb, bw), lambda bi, h, ti: (bi, ti, g * nh + h))

    return pl.pallas_call(
        functools.partial(_hg_kernel, chunk=chunk, layer=layer, heads=heads),
        out_shape=jax.ShapeDtypeStruct((b, t, w), BF16),
        grid=(b, nh, t // tb),
        in_specs=[col(0), col(1), col(2), col(3),
                  pl.BlockSpec((depth, bw), lambda bi, h, ti: (0, h)),
                  pl.BlockSpec((1, HG_HEAD_DIM), lambda bi, h, ti: (0, 0))],
        out_specs=pl.BlockSpec((1, tb, bw), lambda bi, h, ti: (bi, ti, h)),
        scratch_shapes=[pltpu.VMEM((heads, HG_HEAD_DIM, HG_HEAD_DIM), F32)],
        compiler_params=_cparams("parallel", "parallel", "arbitrary"),
        name="hgrn2",
    )(proj_hg, proj_hg, proj_hg, proj_hg, lb_param.astype(F32),
      norm_g.reshape(1, HG_HEAD_DIM).astype(F32))


RW_GROUP_HEADS = 4
RW_GROUP = RW_GROUP_HEADS * RW_HEAD_DIM


def _head_ones(n, head):
    r = lax.broadcasted_iota(jnp.int32, (n, n), 0)
    c = lax.broadcasted_iota(jnp.int32, (n, n), 1)
    return jnp.where((r // head) == (c // head), 1.0, 0.0).astype(BF16)


def _head_sums(x, ones_bd):
    n = ones_bd.shape[0]
    rows, width = x.shape
    if width == n:
        return _dot_exact_lhs(x, ones_bd, parts=2)
    stacked = jnp.concatenate([x[:, s:s + n] for s in range(0, width, n)], axis=0)
    sums = _dot_exact_lhs(stacked, ones_bd, parts=2)
    return jnp.concatenate([sums[g * rows:(g + 1) * rows] for g in range(width // n)], axis=1)


def _softplus(y):
    return jnp.maximum(y, 0.0) + jnp.log(1.0 + jnp.exp(-jnp.abs(y)))


def _rw_prepare_block(i, x_ref, xp_ref, lo_ref, lop_ref, mu_ref, mulo_ref, w0_ref, a0_ref, kk_ref,
                      ka_ref, rk_ref, w2w_ref, w2a_ref, w2g_ref,
                      r_out, lw_out, k_out, v_out, kk_out, kb_out, g_out, bonus_out):
    width = r_out.shape[1]

    def token_shift(cur_ref, prev_ref, mix_ref):
        x = cur_ref[0]
        prev_row = jnp.where(i == 0, 0.0, prev_ref[0, 7:8, :])
        row = lax.broadcasted_iota(jnp.int32, x.shape, 0)
        prev = jnp.where(row == 0, prev_row, pltpu.roll(x, 1, axis=0))
        return x + mix_ref[...] * (prev - x)

    xs = token_shift(x_ref, xp_ref, mu_ref)
    lora = token_shift(lo_ref, lop_ref, mulo_ref)
    w_log = -_softplus(-(w0_ref[...] + _dot(jnp.tanh(lora).astype(BF16), w2w_ref[...]))) - 0.5
    log_decay = -jnp.exp(w_log)
    a = jax.nn.sigmoid(a0_ref[...] + _dot(lora.astype(BF16), w2a_ref[...]))
    g = _dot(jax.nn.sigmoid(lora).astype(BF16), w2g_ref[...])

    r = xs[:, :width]
    k = xs[:, width:2 * width]
    v = xs[:, 2 * width:3 * width]
    ones_bd = _head_ones(RW_GROUP, RW_HEAD_DIM)
    kk = k * kk_ref[...]
    kk = kk * lax.rsqrt(jnp.maximum(_head_sums(kk * kk, ones_bd), 1e-24))
    k = k * (1.0 + (a - 1.0) * ka_ref[...])
    bonus = _head_sums(r * k * rk_ref[...], ones_bd) * v

    r_out[...] = r
    lw_out[...] = log_decay
    k_out[...] = k
    v_out[...] = v
    kk_out[...] = kk
    kb_out[...] = kk * a
    g_out[...] = g
    bonus_out[...] = bonus


def _tile_rows(x, n):
    return jnp.concatenate([x] * n, axis=0)


def _rw_kernel(x_ref, xp_ref, lo_ref, lop_ref, mu_ref, mulo_ref, w0_ref, a0_ref, kkp_ref, ka_ref,
               rk_ref, w2w_ref, w2a_ref, w2g_ref, lnw_ref, lnb_ref, o_ref,
               ht_ref, r_ref, lw_ref, k_ref, v_ref, kk_ref, kb_ref, g_ref, bonus_ref,
               *, chunk, inv_parts):
    nh, hd, gw = RW_GROUP_HEADS, RW_HEAD_DIM, RW_GROUP
    wide = nh * chunk
    tb, width = r_ref.shape
    groups = width // gw
    ti = pl.program_id(1)

    @pl.when(ti == 0)
    def _():
        ht_ref[...] = jnp.zeros_like(ht_ref)

    _rw_prepare_block(ti, x_ref, xp_ref, lo_ref, lop_ref, mu_ref, mulo_ref, w0_ref, a0_ref, kkp_ref,
                      ka_ref, rk_ref, w2w_ref, w2a_ref, w2g_ref,
                      r_ref, lw_ref, k_ref, v_ref, kk_ref, kb_ref, g_ref, bonus_ref)

    def iota(shape, d):
        return lax.broadcasted_iota(jnp.int32, shape, d)

    ltri = jnp.where(iota((chunk, chunk), 1) <= iota((chunk, chunk), 0), 1.0, 0.0).astype(BF16)
    row_cw = iota((chunk, wide), 0)
    pos_cw = iota((chunk, wide), 1) % chunk
    strict = pos_cw < row_cw
    incl = pos_cw <= row_cw
    eye_t = jnp.where(pos_cw == row_cw, 1.0, 0.0)
    off_diag = []
    s = 1
    while s < chunk:
        off_diag.append(((row_cw // (2 * s)) == (pos_cw // (2 * s)))
                        & ((row_cw % (2 * s)) >= s) & ((pos_cw % (2 * s)) < s))
        s *= 2
    bd_wk = (iota((wide, gw), 0) // chunk) == (iota((wide, gw), 1) // hd)
    bd_ww = (iota((wide, wide), 0) // chunk) == (iota((wide, wide), 1) // chunk)
    head_mask = (iota((gw, gw), 0) // hd) == (iota((gw, gw), 1) // hd)
    ones_bd = _head_ones(gw, hd)

    def expand_k(x):
        return jnp.where(bd_wk, _tile_rows(x, nh), 0.0).astype(BF16)

    def expand_w(x):
        return jnp.where(bd_ww, _tile_rows(x, nh), 0.0)

    def mm(a, b):
        if inv_parts == 1:
            return _dot(a.astype(BF16), b.astype(BF16))
        ah, al = _split2(a)
        bh, bl = _split2(b)
        return _dot(ah, bh) + (_dot(ah, bl) + _dot(al, bh))

    def step(c, _):
        rows = pl.ds(pl.multiple_of(c * chunk, chunk), chunk)
        gs = range(groups)
        lanes = [slice(gi * gw, (gi + 1) * gw) for gi in gs]
        lw = lw_ref[rows, :]
        gam = _dot_exact_rhs01(ltri, lw, parts=2)
        gam_end = gam[chunk - 1:chunk, :]
        e_neg = jnp.exp(-gam)
        e_suf = jnp.exp(gam_end - gam)
        k_all = k_ref[rows, :]
        kb_all = kb_ref[rows, :]
        v_all = v_ref[rows, :]
        a_t = -kk_ref[rows, :] * jnp.exp(gam - lw)
        r_t = r_ref[rows, :] * jnp.exp(gam)
        b_t = kb_all * e_neg
        k_t = k_all * e_neg
        b_h = kb_all * e_suf
        k_h = k_all * e_suf
        h_decay = jnp.exp(gam_end)

        lhs = [jnp.concatenate([a_t[:, l], r_t[:, l]], axis=0).astype(BF16) for l in lanes]
        sb = [_dot_nt(lhs[g], expand_k(b_t[:, lanes[g]])) for g in gs]
        sk = [_dot_nt(lhs[g], expand_k(k_t[:, lanes[g]])) for g in gs]
        a_ab = [jnp.where(strict, s[:chunk], 0.0) for s in sb]
        a_rb = [jnp.where(incl, s[chunk:], 0.0) for s in sb]
        a_ak = [jnp.where(strict, s[:chunk], 0.0) for s in sk]
        a_rk = [jnp.where(incl, s[chunk:], 0.0) for s in sk]

        hts = [ht_ref[g] for g in gs]
        from_state = [_dot_nt(lhs[g], hts[g].astype(BF16)) for g in gs]
        from_v = [_dot(jnp.concatenate([a_ak[g], a_rk[g]], axis=0).astype(BF16),
                       expand_k(v_all[:, lanes[g]])) for g in gs]

        p_acc = [eye_t + jnp.where(off_diag[0], a, 0.0) for a in a_ab]
        for li in range(1, len(off_diag)):
            left = [mm(p_acc[g], expand_w(jnp.where(off_diag[li], a_ab[g], 0.0))) for g in gs]
            p_acc = [p_acc[g] + mm(left[g], expand_w(p_acc[g])) for g in gs]

        u = [_dot(p_acc[g].astype(BF16), expand_k(from_state[g][:chunk] + from_v[g][:chunk]))
             for g in gs]
        y = [from_state[g][chunk:] + from_v[g][chunk:] + _dot(a_rb[g].astype(BF16), expand_k(u[g]))
             for g in gs]
        upd = [_dot_tn(jnp.concatenate([u[g], v_all[:, lanes[g]]], axis=0).astype(BF16),
                       jnp.concatenate([b_h[:, lanes[g]], k_h[:, lanes[g]]], axis=0).astype(BF16))
               for g in gs]
        for g in gs:
            ht_ref[g] = hts[g] * h_decay[:, lanes[g]] + jnp.where(head_mask, upd[g], 0.0)

        y_all = jnp.concatenate(y, axis=0)
        yc_all = y_all - _head_sums(y_all, ones_bd) * (1.0 / hd)
        var_all = _head_sums(yc_all * yc_all, ones_bd) * (1.0 / hd)
        yn_all = yc_all * lax.rsqrt(var_all + RW_GN_EPS)
        for g in gs:
            l = lanes[g]
            yn = yn_all[g * chunk:(g + 1) * chunk] * lnw_ref[:, l] + lnb_ref[:, l]
            out = (yn + bonus_ref[rows, l]) * g_ref[rows, l]
            o_ref[0, rows, l] = out.astype(o_ref.dtype)
        return 0

    lax.fori_loop(0, tb // chunk, step, 0, unroll=2)


def rwkv7(rkv, lora, mu, w0, w_w2, a0, w_a2, w_g2, k_k, k_a, r_k, lnx_w, lnx_b, *, chunk=64, tb=256,
          inv_parts=1):
    b, t, cols = rkv.shape
    width = cols // 3
    tb = min(tb, t)

    def padded(w2, first_row):
        full = jnp.zeros((RW_LORA_COLS, width), F32)
        return lax.dynamic_update_slice(full, w2.astype(F32), (first_row, 0)).astype(BF16)

    w2w = padded(w_w2, 0)
    w2a = padded(w_a2, RW_DECAY_RANK)
    w2g = padded(w_g2, RW_DECAY_RANK + RW_AAA_RANK)

    def vec(p):
        return p.reshape(1, -1).astype(F32)

    def vspec(n):
        return pl.BlockSpec((1, n), lambda bi, i: (0, 0))

    def cur(n):
        return pl.BlockSpec((1, tb, n), lambda bi, i: (bi, i, 0))

    def prev(n):
        return pl.BlockSpec((1, 8, n), lambda bi, i: (bi, jnp.maximum(i * (tb // 8) - 1, 0), 0))

    wspec = pl.BlockSpec((RW_LORA_COLS, width), lambda bi, i: (0, 0))
    return pl.pallas_call(
        functools.partial(_rw_kernel, chunk=chunk, inv_parts=inv_parts),
        out_shape=jax.ShapeDtypeStruct((b, t, width), BF16),
        grid=(b, t // tb),
        in_specs=[cur(cols), prev(cols), cur(RW_LORA_COLS), prev(RW_LORA_COLS),
                  vspec(cols), vspec(RW_LORA_COLS),
                  vspec(width), vspec(width), vspec(width), vspec(width), vspec(width),
                  wspec, wspec, wspec, vspec(width), vspec(width)],
        out_specs=cur(width),
        scratch_shapes=[pltpu.VMEM((width // RW_GROUP, RW_GROUP, RW_GROUP), F32)]
                       + [pltpu.VMEM((tb, width), F32)] * 8,
        compiler_params=_cparams("parallel", "arbitrary"),
        name="rwkv7",
    )(rkv, rkv, lora, lora, vec(mu[:cols]), vec(mu[cols:]), vec(w0), vec(a0), vec(k_k), vec(k_a),
      vec(r_k), w2w, w2a, w2g, vec(lnx_w), vec(lnx_b))


def kernel(x, norm1_g, w_in, sb_norm_g, hg_lb_param, hg_norm_g, rw_mu, rw_w0, rw_w_w2, rw_a0,
           rw_w_a2, rw_w_g2, rw_k_k, rw_k_a, rw_r_k, rw_lnx_w, rw_lnx_b, w_out, norm2_g,
           w_ff_in, w_ff_out, final_g):
    b, t, d = x.shape
    depth = norm1_g.shape[0]
    sb_w = sb_norm_g.shape[1]
    hg_w = hg_lb_param.shape[1]
    rw_w = rw_lnx_w.shape[1]
    sb_cols, hg_cols, rkv_cols = 3 * sb_w, 4 * hg_w, 3 * rw_w
    m = b * t
    xf = x.reshape(m, d)
    w_in_b, w_out_b = w_in.astype(BF16), w_out.astype(BF16)
    w_ff_in_b, w_ff_out_b = w_ff_in.astype(BF16), w_ff_out.astype(BF16)
    h, ss = rmsnorm(xf, norm1_g[0], BF16), None
    for l in range(depth):
        proj = functools.partial(matmul, h, w_in_b, layer=l, bm=1024, row_ss=ss)
        p_sb = proj(bn=1024, col0=0, n=sb_cols, out_dtype=BF16, name="proj_sb").reshape(b, t, -1)
        p_hg = proj(bn=1024, col0=sb_cols, n=hg_cols, name="proj_hg").reshape(b, t, -1)
        p_rkv = proj(bn=1024, col0=sb_cols + hg_cols, n=rkv_cols, name="proj_rkv").reshape(b, t, -1)
        p_lora = proj(bn=RW_LORA_COLS, col0=sb_cols + hg_cols + rkv_cols, n=RW_LORA_COLS,
                      name="proj_lora").reshape(b, t, -1)
        o_sb = sb_attention(p_sb, sb_norm_g[l])
        o_hg = hgrn2(p_hg, hg_lb_param, hg_norm_g[l], layer=l)
        o_rw = rwkv7(p_rkv, p_lora, rw_mu[l], rw_w0[l], rw_w_w2[l], rw_a0[l], rw_w_a2[l],
                     rw_w_g2[l], rw_k_k[l], rw_k_a[l], rw_r_k[l], rw_lnx_w[l], rw_lnx_b[l])
        mix = [o.reshape(m, -1) for o in (o_sb, o_hg, o_rw)]
        xf, h, ss = matmul(mix, w_out_b, layer=l, bm=1024, bn=512, residual=xf,
                           next_norm_g=norm2_g[l], name="out_proj")
        ff = matmul(h, w_ff_in_b, layer=l, bm=1024, bn=1024, relu2=True, row_ss=ss, out_dtype=BF16,
                    name="ff_in")
        ff_out = functools.partial(matmul, ff, w_ff_out_b, layer=l, bm=1024, bn=1024, bk=2048,
                                   residual=xf)
        if l + 1 < depth:
            xf, h, ss = ff_out(next_norm_g=norm1_g[l + 1], name="ff_out_norm")
        else:
            xf = ff_out(name="ff_out")
    return rmsnorm(xf, final_g, F32).reshape(b, t, d)
```

```python
import functools
import math

import jax
import jax.numpy as jnp
import numpy as np
from jax import lax
from jax.experimental import pallas as pl
from jax.experimental.pallas import tpu as pltpu

F32 = jnp.float32
BF16 = jnp.bfloat16

NORM_EPS = 1e-5
V7X_VMEM_BYTES = 64 * 1024 * 1024
VMEM_LIMIT = 56 * 1024 * 1024
LANES = 128

SB_HEAD_DIM = 128
SB_LOG_WEIGHT_CUTOFF = -90.0
HG_HEAD_DIM = 128
HG_MIN_F = 1e-30
RW_HEAD_DIM = 64
RW_GN_EPS = 64e-5
RW_DECAY_RANK = 96
RW_AAA_RANK = 96
RW_GATE_RANK = 64
RW_LORA_COLS = RW_DECAY_RANK + RW_AAA_RANK + RW_GATE_RANK


def _cparams(*sem):
    return pltpu.CompilerParams(dimension_semantics=sem, vmem_limit_bytes=VMEM_LIMIT)


def _dot(a, b):
    return jnp.dot(a, b, preferred_element_type=F32)


def _dot_nt(a, b):
    return lax.dot_general(a, b, (((1,), (1,)), ((), ())), preferred_element_type=F32)


def _dot_tn(a, b):
    return lax.dot_general(a, b, (((0,), (0,)), ((), ())), preferred_element_type=F32)


def _split2(x):
    hi = x.astype(BF16)
    lo = (x - hi.astype(F32)).astype(BF16)
    return hi, lo


def _split3(x):
    hi = x.astype(BF16)
    r1 = x - hi.astype(F32)
    mid = r1.astype(BF16)
    lo = (r1 - mid.astype(F32)).astype(BF16)
    return hi, mid, lo


def _dot_exact_rhs01(a01, x, parts=3):
    ps = _split3(x) if parts == 3 else _split2(x)
    out = _dot(a01, ps[0])
    for p in ps[1:]:
        out = out + _dot(a01, p)
    return out


def _dot_exact_lhs(x, b01, parts=2):
    ps = _split3(x) if parts == 3 else _split2(x)
    out = _dot(ps[0], b01)
    for p in ps[1:]:
        out = out + _dot(p, b01)
    return out


def _rmsnorm_kernel(x_ref, g_ref, o_ref):
    x = x_ref[...]
    ms = jnp.mean(x * x, axis=-1, keepdims=True)
    o_ref[...] = (x * lax.rsqrt(ms + NORM_EPS) * g_ref[...]).astype(o_ref.dtype)


def rmsnorm(x, g, out_dtype, bm=512):
    m, d = x.shape
    bm = min(bm, m)
    return pl.pallas_call(
        _rmsnorm_kernel,
        out_shape=jax.ShapeDtypeStruct((m, d), out_dtype),
        grid=(m // bm,),
        in_specs=[pl.BlockSpec((bm, d), lambda i: (i, 0)),
                  pl.BlockSpec((1, d), lambda i: (0, 0))],
        out_specs=pl.BlockSpec((bm, d), lambda i: (i, 0)),
        compiler_params=_cparams("parallel"),
        name="rmsnorm",
    )(x, g.reshape(1, d).astype(F32))


def _matmul_kernel(*refs, n_lhs, nk, relu2, has_res, has_row_ss, emit_norm, has_side, norm_dim):
    refs = list(refs)
    a_refs = [refs.pop(0) for _ in range(n_lhs)]
    b_ref = refs.pop(0)
    r_ref = refs.pop(0) if has_res else None
    ss_in_ref = refs.pop(0) if has_row_ss else None
    gnext_ref = refs.pop(0) if emit_norm else None
    side_in_ref = refs.pop(0) if has_side else None
    o_ref = refs.pop(0)
    xg_ref, ss_out_ref = (refs.pop(0), refs.pop(0)) if emit_norm else (None, None)
    side_out_ref = refs.pop(0) if has_side else None
    rest = refs
    j = pl.program_id(1)

    if has_side:
        side_out_ref[...] = side_in_ref[...].astype(side_out_ref.dtype)

    def product():
        acc, k0 = None, 0
        for a_ref in a_refs:
            kw = a_ref.shape[1]
            part = _dot(a_ref[...], b_ref[k0:k0 + kw, :])
            acc = part if acc is None else acc + part
            k0 += kw
        return acc

    def finish(acc):
        if has_row_ss:
            acc = acc * lax.rsqrt(ss_in_ref[:, :1] * (1.0 / norm_dim) + NORM_EPS)
        if relu2:
            acc = jnp.square(jnp.maximum(acc, 0.0))
        if has_res:
            acc = acc + r_ref[...]
        o_ref[...] = acc.astype(o_ref.dtype)
        if emit_norm:
            xg_ref[...] = (acc * gnext_ref[...]).astype(xg_ref.dtype)
            part = jnp.broadcast_to(jnp.sum(acc * acc, axis=-1, keepdims=True), ss_out_ref.shape)

            @pl.when(j == 0)
            def _():
                ss_out_ref[...] = part

            @pl.when(j > 0)
            def _():
                ss_out_ref[...] += part

    if nk == 1:
        finish(product())
        return

    acc_ref = rest[0]
    k = pl.program_id(2)

    @pl.when(k == 0)
    def _():
        acc_ref[...] = product()

    @pl.when((k > 0) & (k < nk - 1))
    def _():
        acc_ref[...] += product()

    @pl.when(k == nk - 1)
    def _():
        finish(acc_ref[...] + product())


def matmul(a, b, *, bm, bn, bk=None, layer=0, col0=0, n=None, relu2=False, residual=None,
           row_ss=None, next_norm_g=None, side_cast=None, out_dtype=F32, name="matmul"):
    a_list = list(a) if isinstance(a, (list, tuple)) else [a]
    m = a_list[0].shape[0]
    kdim = sum(x.shape[1] for x in a_list)
    n = b.shape[2] - col0 if n is None else n
    bm, bn = min(bm, m), min(bn, n)
    bk = kdim if bk is None else min(bk, kdim)
    assert b.shape[1] == kdim and m % bm == 0 and n % bn == 0 and kdim % bk == 0 and col0 % bn == 0
    nk = kdim // bk
    assert nk == 1 or len(a_list) == 1
    has_res = residual is not None
    jb0 = col0 // bn
    if len(a_list) == 1:
        in_specs = [pl.BlockSpec((bm, bk), lambda i, j, k: (i, k))]
    else:
        in_specs = [pl.BlockSpec((bm, x.shape[1]), lambda i, j, k: (i, 0)) for x in a_list]
    in_specs.append(pl.BlockSpec((None, bk, bn), lambda i, j, k: (layer, k, jb0 + j)))
    args = a_list + [b]
    if has_res:
        in_specs.append(pl.BlockSpec((bm, bn), lambda i, j, k: (i, j)))
        args.append(residual)
    stat_spec = pl.BlockSpec((bm, LANES), lambda i, j, k: (i, 0))
    if row_ss is not None:
        in_specs.append(stat_spec)
        args.append(row_ss)
    emit_norm = next_norm_g is not None
    out_shape = jax.ShapeDtypeStruct((m, n), out_dtype)
    out_specs = pl.BlockSpec((bm, bn), lambda i, j, k: (i, j))
    if emit_norm:
        assert col0 == 0 and n == b.shape[2]
        in_specs.append(pl.BlockSpec((1, bn), lambda i, j, k: (0, j)))
        args.append(next_norm_g.reshape(1, n).astype(F32))
        out_shape = [out_shape, jax.ShapeDtypeStruct((m, n), BF16),
                     jax.ShapeDtypeStruct((m, LANES), F32)]
        out_specs = [out_specs, pl.BlockSpec((bm, bn), lambda i, j, k: (i, j)), stat_spec]
    else:
        out_shape, out_specs = [out_shape], [out_specs]
    grid = (m // bm, n // bn, nk)
    if side_cast is not None:
        src, src_layer = side_cast
        steps = grid[0] * grid[1] * grid[2]
        per_step = src[0].size // steps
        side_rows = 16
        side_cols = per_step // side_rows
        assert src[0].size == steps * side_rows * side_cols and side_cols % LANES == 0
        side_spec = pl.BlockSpec((None, side_rows, side_cols),
                                 lambda i, j, k: (src_layer, (i * grid[1] + j) * grid[2] + k, 0))
        in_specs.append(side_spec)
        args.append(src.reshape(src.shape[0], steps * side_rows, side_cols))
        out_shape.append(jax.ShapeDtypeStruct((1, steps * side_rows, side_cols), BF16))
        out_specs.append(pl.BlockSpec((None, side_rows, side_cols),
                                      lambda i, j, k: (0, (i * grid[1] + j) * grid[2] + k, 0)))
    outs = pl.pallas_call(
        functools.partial(_matmul_kernel, n_lhs=len(a_list), nk=nk, relu2=relu2, has_res=has_res,
                          has_row_ss=row_ss is not None, emit_norm=emit_norm,
                          has_side=side_cast is not None, norm_dim=kdim),
        out_shape=out_shape,
        grid=grid,
        in_specs=in_specs,
        out_specs=out_specs,
        scratch_shapes=[pltpu.VMEM((bm, bn), F32)] if nk > 1 else [],
        compiler_params=_cparams("parallel", "arbitrary" if emit_norm else "parallel", "arbitrary"),
        name=name,
    )(*args)
    if side_cast is not None:
        outs = list(outs[:-1]) + [outs[-1].reshape((1,) + side_cast[0].shape[1:])]
    return outs[0] if len(outs) == 1 else tuple(outs)


def _sb_kernel(q_ref, k_ref, v_ref, g_ref, o_ref, *, blk, heads):
    i = pl.program_id(2)
    scale = SB_HEAD_DIM ** -0.5
    hd = SB_HEAD_DIM
    qs = [q_ref[0, :, h * hd:(h + 1) * hd].astype(BF16) for h in range(heads)]
    row = lax.broadcasted_iota(jnp.int32, (blk, blk), 0)
    col = lax.broadcasted_iota(jnp.int32, (blk, blk), 1)
    before = col < row
    r2 = lax.broadcasted_iota(jnp.int32, (blk, 2 * blk), 0)
    c2 = lax.broadcasted_iota(jnp.int32, (blk, 2 * blk), 1)
    later = jnp.where((r2 > c2) | (c2 >= blk), 1.0, 0.0).astype(BF16)

    def blocks(specs, carry):
        hs = range(heads)
        units = [(s, h) for s in range(len(specs)) for h in hs]
        starts = [pl.multiple_of(j * blk, blk) for j, _ in specs]
        kbs = [k_ref[0, pl.ds(starts[s], blk), h * hd:(h + 1) * hd].astype(BF16) for s, h in units]
        vbs = [v_ref[0, pl.ds(starts[s], blk), h * hd:(h + 1) * hd].astype(BF16) for s, h in units]
        zs = [_dot_nt(qs[h], kbs[u]) * scale for u, (s, h) in enumerate(units)]
        log_betas = [jnp.minimum(z, 0.0) - jnp.log(1.0 + jnp.exp(-jnp.abs(z))) for z in zs]
        log_keeps = [lb - z for lb, z in zip(log_betas, zs)]
        log_keeps = [lk if specs[s][1] is None else jnp.where(specs[s][1], lk, 0.0)
                     for lk, (s, h) in zip(log_keeps, units)]
        cs_all = _dot_exact_lhs(jnp.concatenate(log_keeps, axis=0), later, parts=2)
        css = [cs_all[u * blk:(u + 1) * blk] for u in range(len(units))]
        accs = [carry[2 * h] for h in hs]
        cums = [carry[2 * h + 1] for h in hs]
        for u, (s, h) in enumerate(units):
            w = jnp.exp(log_betas[u] + cums[h] + css[u][:, :blk])
            if specs[s][1] is not None:
                w = jnp.where(specs[s][1], w, 0.0)
            accs[h] = accs[h] + _dot(w.astype(BF16), vbs[u])
            cums[h] = cums[h] + css[u][:, blk:]
        out = []
        for h in hs:
            out += [accs[h], cums[h]]
        return tuple(out)

    carry = (jnp.zeros((blk, hd), F32), jnp.zeros((blk, blk), F32)) * heads
    has_previous = (jnp.zeros((blk, blk), jnp.int32) + i) > 0
    carry = blocks([(i, before), (jnp.maximum(i - 1, 0), has_previous)], carry)

    def alive(state):
        j = state[0]
        top = state[2]
        for h in range(1, heads):
            top = jnp.maximum(top, state[2 + 2 * h])
        return (j >= 0) & (jnp.max(top) > SB_LOG_WEIGHT_CUTOFF)

    def body(state):
        j = state[0]
        return (j - 1,) + blocks([(j, None)], state[1:])

    state = lax.while_loop(alive, body, (i - 2,) + carry)
    for h in range(heads):
        acc = state[1 + 2 * h]
        ms = jnp.mean(acc * acc, axis=-1, keepdims=True)
        o_ref[0, :, h * hd:(h + 1) * hd] = (
            acc * lax.rsqrt(ms + NORM_EPS) * g_ref[:, h * hd:(h + 1) * hd]).astype(o_ref.dtype)


def sb_attention(proj_sb, norm_g, *, blk=128, heads=8):
    b, t, w3 = proj_sb.shape
    w = w3 // 3
    bw = heads * SB_HEAD_DIM
    nh = w // bw
    blk = min(blk, t)
    return pl.pallas_call(
        functools.partial(_sb_kernel, blk=blk, heads=heads),
        out_shape=jax.ShapeDtypeStruct((b, t, w), BF16),
        grid=(b, nh, t // blk),
        in_specs=[pl.BlockSpec((1, blk, bw), lambda bi, h, i: (bi, i, h)),
                  pl.BlockSpec((1, t, bw), lambda bi, h, i: (bi, 0, nh + h)),
                  pl.BlockSpec((1, t, bw), lambda bi, h, i: (bi, 0, 2 * nh + h)),
                  pl.BlockSpec((1, bw), lambda bi, h, i: (0, h))],
        out_specs=pl.BlockSpec((1, blk, bw), lambda bi, h, i: (bi, i, h)),
        compiler_params=_cparams("parallel", "parallel", "arbitrary"),
        name="sb_attention",
    )(proj_sb, proj_sb, proj_sb, norm_g.reshape(1, w).astype(F32))


def _hg_level_halves(chunk):
    halves = []
    m = chunk // 2
    while m >= 1:
        halves.append(m)
        m //= 2
    return halves


HG_ROW_TILE = 8


def _hg_decay_selectors(chunk):
    t = lax.broadcasted_iota(jnp.int32, (chunk, chunk), 0)
    j = lax.broadcasted_iota(jnp.int32, (chunk, chunk), 1)
    mats = [j <= t]
    for m in _hg_level_halves(chunk):
        if m >= HG_ROW_TILE:
            continue
        mid = (t // (2 * m)) * (2 * m) + m - 1
        upper = (t % (2 * m)) >= m
        mats.append((upper & (j > mid) & (j <= t)) | (~upper & (j > t) & (j <= mid)))
    return jnp.concatenate([jnp.where(mm, 1.0, 0.0).astype(BF16) for mm in mats], axis=0)


def _hg_level_decays(sums, chunk):
    b = sums[0:chunk]
    out, small = [], 0
    for m in _hg_level_halves(chunk):
        if m >= HG_ROW_TILE:
            mids = [jnp.broadcast_to(b[s + m - 1:s + m, :], (2 * m, b.shape[1]))
                    for s in range(0, chunk, 2 * m)]
            b_mid = mids[0] if len(mids) == 1 else jnp.concatenate(mids, axis=0)
            out.append(jnp.exp(-jnp.abs(b - b_mid)))
        else:
            small += 1
            out.append(jnp.exp(sums[small * chunk:(small + 1) * chunk]))
    return out


def _hg_kernel(q_ref, f_ref, i_ref, g_ref, lbp_ref, ng_ref, o_ref, st_ref, *, chunk, layer, heads):
    hd = HG_HEAD_DIM
    tb = q_ref.shape[1]

    @pl.when(pl.program_id(2) == 0)
    def _():
        st_ref[...] = jnp.zeros_like(st_ref)

    halves = _hg_level_halves(chunk)
    sel = _hg_decay_selectors(chunk)

    p = lbp_ref[...]
    e = jnp.exp(p - jnp.max(p, axis=0, keepdims=True))
    probs = e / jnp.sum(e, axis=0, keepdims=True)
    lb = jnp.zeros((1, heads * hd), F32)
    for l in range(1, layer + 1):
        lb = lb + probs[l:l + 1, :]

    trow = lax.broadcasted_iota(jnp.int32, (chunk, heads * hd), 0)
    ts = lax.broadcasted_iota(jnp.int32, (chunk, chunk), 0)
    ss = lax.broadcasted_iota(jnp.int32, (chunk, chunk), 1)
    ng = ng_ref[...]

    def step(c, _):
        start = pl.multiple_of(c * chunk, chunk)
        rows = pl.ds(start, chunk)
        q2 = q_ref[0, rows, :]
        fr = f_ref[0, rows, :]
        v2 = i_ref[0, rows, :]
        gate = g_ref[0, rows, :]
        f = lb + (1.0 - lb) * jax.nn.sigmoid(fr)
        log_f = jnp.log(jnp.maximum(f, HG_MIN_F))
        key2 = (1.0 - lb) * jax.nn.sigmoid(-fr)
        sums = _dot_exact_rhs01(sel, log_f, parts=2)
        b = sums[0:chunk]
        b_last = b[chunk - 1:chunk, :]
        level_decay = _hg_level_decays(sums, chunk)
        silu = gate * jax.nn.sigmoid(gate)

        hs = range(heads)
        lanes = [slice(h * hd, (h + 1) * hd) for h in hs]
        states = [st_ref[h] for h in hs]
        vbs = [v2[:, l].astype(BF16) for l in lanes]
        scores = [None] * heads
        for li, m in enumerate(halves):
            el = level_decay[li]
            upper = (trow % (2 * m)) >= m
            ql = jnp.where(upper, q2 * el, 0.0).astype(BF16)
            kl = jnp.where(upper, 0.0, key2 * el).astype(BF16)
            for h in hs:
                sl = _dot_nt(ql[:, lanes[h]], kl[:, lanes[h]])
                if 2 * m < chunk:
                    sl = jnp.where((ts // (2 * m)) == (ss // (2 * m)), sl, 0.0)
                scores[h] = sl if scores[h] is None else scores[h] + sl
        q_pre = (q2 * jnp.exp(b)).astype(BF16)
        k_suf = (key2 * jnp.exp(b_last - b)).astype(BF16)
        qk = q2 * key2
        inter = [_dot_nt(q_pre[:, lanes[h]], states[h].astype(BF16)) for h in hs]
        intra = [_dot(scores[h].astype(BF16), vbs[h]) for h in hs]
        upd = [_dot_tn(vbs[h], k_suf[:, lanes[h]]) for h in hs]
        decay_all = jnp.exp(b_last)
        for h in hs:
            l = lanes[h]
            st_ref[h] = states[h] * decay_all[:, l] + upd[h]
            diag = jnp.sum(qk[:, l], axis=-1, keepdims=True)
            o = intra[h] + diag * v2[:, l] + inter[h]
            ms = jnp.mean(o * o, axis=-1, keepdims=True)
            o = o * lax.rsqrt(ms + NORM_EPS) * ng
            o_ref[0, rows, l] = (o * silu[:, l]).astype(o_ref.dtype)
        return 0

    lax.fori_loop(0, tb // chunk, step, 0, unroll=2)


def hgrn2(proj_hg, lb_param, norm_g, *, layer, chunk=128, heads=4, tb=512):
    b, t, w4 = proj_hg.shape
    w = w4 // 4
    bw = heads * HG_HEAD_DIM
    nh = w // bw
    tb = min(tb, t)
    chunk = min(chunk, tb)
    depth = lb_param.shape[0]

    def col(g):
        return pl.BlockSpec((1, tb, bw), lambda bi, h, ti: (bi, ti, g * nh + h))

    return pl.pallas_call(
        functools.partial(_hg_kernel, chunk=chunk, layer=layer, heads=heads),
        out_shape=jax.ShapeDtypeStruct((b, t, w), BF16),
        grid=(b, nh, t // tb),
        in_specs=[col(0), col(1), col(2), col(3),
                  pl.BlockSpec((depth, bw), lambda bi, h, ti: (0, h)),
                  pl.BlockSpec((1, HG_HEAD_DIM), lambda bi, h, ti: (0, 0))],
        out_specs=pl.BlockSpec((1, tb, bw), lambda bi, h, ti: (bi, ti, h)),
        scratch_shapes=[pltpu.VMEM((heads, HG_HEAD_DIM, HG_HEAD_DIM), F32)],
        compiler_params=_cparams("parallel", "parallel", "arbitrary"),
        name="hgrn2",
    )(proj_hg, proj_hg, proj_hg, proj_hg, lb_param.astype(F32),
      norm_g.reshape(1, HG_HEAD_DIM).astype(F32))


RW_GROUP_HEADS = 4
RW_GROUP = RW_GROUP_HEADS * RW_HEAD_DIM


def _head_ones(n, head):
    r = lax.broadcasted_iota(jnp.int32, (n, n), 0)
    c = lax.broadcasted_iota(jnp.int32, (n, n), 1)
    return jnp.where((r // head) == (c // head), 1.0, 0.0).astype(BF16)


def _head_sums(x, ones_bd):
    n = ones_bd.shape[0]
    rows, width = x.shape
    if width == n:
        return _dot_exact_lhs(x, ones_bd, parts=2)
    stacked = jnp.concatenate([x[:, s:s + n] for s in range(0, width, n)], axis=0)
    sums = _dot_exact_lhs(stacked, ones_bd, parts=2)
    return jnp.concatenate([sums[g * rows:(g + 1) * rows] for g in range(width // n)], axis=1)


def _softplus(y):
    return jnp.maximum(y, 0.0) + jnp.log(1.0 + jnp.exp(-jnp.abs(y)))


def _rw_prepare_block(i, x_ref, xp_ref, lo_ref, lop_ref, mu_ref, mulo_ref, w0_ref, a0_ref, kk_ref,
                      ka_ref, rk_ref, w2w_ref, w2a_ref, w2g_ref,
                      r_out, lw_out, k_out, v_out, kk_out, kb_out, g_out, bonus_out):
    width = r_out.shape[1]

    def token_shift(cur_ref, prev_ref, mix_ref):
        x = cur_ref[0]
        prev_row = jnp.where(i == 0, 0.0, prev_ref[0, 7:8, :])
        row = lax.broadcasted_iota(jnp.int32, x.shape, 0)
        prev = jnp.where(row == 0, prev_row, pltpu.roll(x, 1, axis=0))
        return x + mix_ref[...] * (prev - x)

    xs = token_shift(x_ref, xp_ref, mu_ref)
    lora = token_shift(lo_ref, lop_ref, mulo_ref)
    w_log = -_softplus(-(w0_ref[...] + _dot(jnp.tanh(lora).astype(BF16), w2w_ref[...]))) - 0.5
    log_decay = -jnp.exp(w_log)
    a = jax.nn.sigmoid(a0_ref[...] + _dot(lora.astype(BF16), w2a_ref[...]))
    g = _dot(jax.nn.sigmoid(lora).astype(BF16), w2g_ref[...])

    r = xs[:, :width]
    k = xs[:, width:2 * width]
    v = xs[:, 2 * width:3 * width]
    ones_bd = _head_ones(RW_GROUP, RW_HEAD_DIM)
    kk = k * kk_ref[...]
    kk = kk * lax.rsqrt(jnp.maximum(_head_sums(kk * kk, ones_bd), 1e-24))
    k = k * (1.0 + (a - 1.0) * ka_ref[...])
    bonus = _head_sums(r * k * rk_ref[...], ones_bd) * v

    r_out[...] = r
    lw_out[...] = log_decay
    k_out[...] = k
    v_out[...] = v
    kk_out[...] = kk
    kb_out[...] = kk * a
    g_out[...] = g
    bonus_out[...] = bonus


def _tile_rows(x, n):
    return jnp.concatenate([x] * n, axis=0)


def _rw_kernel(x_ref, xp_ref, lo_ref, lop_ref, mu_ref, mulo_ref, w0_ref, a0_ref, kkp_ref, ka_ref,
               rk_ref, w2w_ref, w2a_ref, w2g_ref, lnw_ref, lnb_ref, o_ref,
               ht_ref, r_ref, lw_ref, k_ref, v_ref, kk_ref, kb_ref, g_ref, bonus_ref,
               *, chunk, inv_parts):
    nh, hd, gw = RW_GROUP_HEADS, RW_HEAD_DIM, RW_GROUP
    wide = nh * chunk
    tb, width = r_ref.shape
    groups = width // gw
    ti = pl.program_id(1)

    @pl.when(ti == 0)
    def _():
        ht_ref[...] = jnp.zeros_like(ht_ref)

    _rw_prepare_block(ti, x_ref, xp_ref, lo_ref, lop_ref, mu_ref, mulo_ref, w0_ref, a0_ref, kkp_ref,
                      ka_ref, rk_ref, w2w_ref, w2a_ref, w2g_ref,
                      r_ref, lw_ref, k_ref, v_ref, kk_ref, kb_ref, g_ref, bonus_ref)

    def iota(shape, d):
        return lax.broadcasted_iota(jnp.int32, shape, d)

    ltri = jnp.where(iota((chunk, chunk), 1) <= iota((chunk, chunk), 0), 1.0, 0.0).astype(BF16)
    row_cw = iota((chunk, wide), 0)
    pos_cw = iota((chunk, wide), 1) % chunk
    strict = pos_cw < row_cw
    incl = pos_cw <= row_cw
    eye_t = jnp.where(pos_cw == row_cw, 1.0, 0.0)
    off_diag = []
    s = 1
    while s < chunk:
        off_diag.append(((row_cw // (2 * s)) == (pos_cw // (2 * s)))
                        & ((row_cw % (2 * s)) >= s) & ((pos_cw % (2 * s)) < s))
        s *= 2
    bd_wk = (iota((wide, gw), 0) // chunk) == (iota((wide, gw), 1) // hd)
    bd_ww = (iota((wide, wide), 0) // chunk) == (iota((wide, wide), 1) // chunk)
    head_mask = (iota((gw, gw), 0) // hd) == (iota((gw, gw), 1) // hd)
    ones_bd = _head_ones(gw, hd)

    def expand_k(x):
        return jnp.where(bd_wk, _tile_rows(x, nh), 0.0).astype(BF16)

    def expand_w(x):
        return jnp.where(bd_ww, _tile_rows(x, nh), 0.0)

    def mm(a, b):
        if inv_parts == 1:
            return _dot(a.astype(BF16), b.astype(BF16))
        ah, al = _split2(a)
        bh, bl = _split2(b)
        return _dot(ah, bh) + (_dot(ah, bl) + _dot(al, bh))

    def step(c, _):
        rows = pl.ds(pl.multiple_of(c * chunk, chunk), chunk)
        gs = range(groups)
        lanes = [slice(gi * gw, (gi + 1) * gw) for gi in gs]
        lw = lw_ref[rows, :]
        gam = _dot_exact_rhs01(ltri, lw, parts=2)
        gam_end = gam[chunk - 1:chunk, :]
        e_neg = jnp.exp(-gam)
        e_suf = jnp.exp(gam_end - gam)
        k_all = k_ref[rows, :]
        kb_all = kb_ref[rows, :]
        v_all = v_ref[rows, :]
        a_t = -kk_ref[rows, :] * jnp.exp(gam - lw)
        r_t = r_ref[rows, :] * jnp.exp(gam)
        b_t = kb_all * e_neg
        k_t = k_all * e_neg
        b_h = kb_all * e_suf
        k_h = k_all * e_suf
        h_decay = jnp.exp(gam_end)

        lhs = [jnp.concatenate([a_t[:, l], r_t[:, l]], axis=0).astype(BF16) for l in lanes]
        sb = [_dot_nt(lhs[g], expand_k(b_t[:, lanes[g]])) for g in gs]
        sk = [_dot_nt(lhs[g], expand_k(k_t[:, lanes[g]])) for g in gs]
        a_ab = [jnp.where(strict, s[:chunk], 0.0) for s in sb]
        a_rb = [jnp.where(incl, s[chunk:], 0.0) for s in sb]
        a_ak = [jnp.where(strict, s[:chunk], 0.0) for s in sk]
        a_rk = [jnp.where(incl, s[chunk:], 0.0) for s in sk]

        hts = [ht_ref[g] for g in gs]
        from_state = [_dot_nt(lhs[g], hts[g].astype(BF16)) for g in gs]
        from_v = [_dot(jnp.concatenate([a_ak[g], a_rk[g]], axis=0).astype(BF16),
                       expand_k(v_all[:, lanes[g]])) for g in gs]

        p_acc = [eye_t + jnp.where(off_diag[0], a, 0.0) for a in a_ab]
        for li in range(1, len(off_diag)):
            left = [mm(p_acc[g], expand_w(jnp.where(off_diag[li], a_ab[g], 0.0))) for g in gs]
            p_acc = [p_acc[g] + mm(left[g], expand_w(p_acc[g])) for g in gs]

        u = [_dot(p_acc[g].astype(BF16), expand_k(from_state[g][:chunk] + from_v[g][:chunk]))
             for g in gs]
        y = [from_state[g][chunk:] + from_v[g][chunk:] + _dot(a_rb[g].astype(BF16), expand_k(u[g]))
             for g in gs]
        upd = [_dot_tn(jnp.concatenate([u[g], v_all[:, lanes[g]]], axis=0).astype(BF16),
                       jnp.concatenate([b_h[:, lanes[g]], k_h[:, lanes[g]]], axis=0).astype(BF16))
               for g in gs]
        for g in gs:
            ht_ref[g] = hts[g] * h_decay[:, lanes[g]] + jnp.where(head_mask, upd[g], 0.0)

        y_all = jnp.concatenate(y, axis=0)
        yc_all = y_all - _head_sums(y_all, ones_bd) * (1.0 / hd)
        var_all = _head_sums(yc_all * yc_all, ones_bd) * (1.0 / hd)
        yn_all = yc_all * lax.rsqrt(var_all + RW_GN_EPS)
        for g in gs:
            l = lanes[g]
            yn = yn_all[g * chunk:(g + 1) * chunk] * lnw_ref[:, l] + lnb_ref[:, l]
            out = (yn + bonus_ref[rows, l]) * g_ref[rows, l]
            o_ref[0, rows, l] = out.astype(o_ref.dtype)
        return 0

    lax.fori_loop(0, tb // chunk, step, 0, unroll=2)


def rwkv7(rkv, lora, mu, w0, w_w2, a0, w_a2, w_g2, k_k, k_a, r_k, lnx_w, lnx_b, *, chunk=64, tb=256,
          inv_parts=1):
    b, t, cols = rkv.shape
    width = cols // 3
    tb = min(tb, t)

    def padded(w2, first_row):
        full = jnp.zeros((RW_LORA_COLS, width), F32)
        return lax.dynamic_update_slice(full, w2.astype(F32), (first_row, 0)).astype(BF16)

    w2w = padded(w_w2, 0)
    w2a = padded(w_a2, RW_DECAY_RANK)
    w2g = padded(w_g2, RW_DECAY_RANK + RW_AAA_RANK)

    def vec(p):
        return p.reshape(1, -1).astype(F32)

    def vspec(n):
        return pl.BlockSpec((1, n), lambda bi, i: (0, 0))

    def cur(n):
        return pl.BlockSpec((1, tb, n), lambda bi, i: (bi, i, 0))

    def prev(n):
        return pl.BlockSpec((1, 8, n), lambda bi, i: (bi, jnp.maximum(i * (tb // 8) - 1, 0), 0))

    wspec = pl.BlockSpec((RW_LORA_COLS, width), lambda bi, i: (0, 0))
    return pl.pallas_call(
        functools.partial(_rw_kernel, chunk=chunk, inv_parts=inv_parts),
        out_shape=jax.ShapeDtypeStruct((b, t, width), BF16),
        grid=(b, t // tb),
        in_specs=[cur(cols), prev(cols), cur(RW_LORA_COLS), prev(RW_LORA_COLS),
                  vspec(cols), vspec(RW_LORA_COLS),
                  vspec(width), vspec(width), vspec(width), vspec(width), vspec(width),
                  wspec, wspec, wspec, vspec(width), vspec(width)],
        out_specs=cur(width),
        scratch_shapes=[pltpu.VMEM((width // RW_GROUP, RW_GROUP, RW_GROUP), F32)]
                       + [pltpu.VMEM((tb, width), F32)] * 8,
        compiler_params=_cparams("parallel", "arbitrary"),
        name="rwkv7",
    )(rkv, rkv, lora, lora, vec(mu[:cols]), vec(mu[cols:]), vec(w0), vec(a0), vec(k_k), vec(k_a),
      vec(r_k), w2w, w2a, w2g, vec(lnx_w), vec(lnx_b))


def kernel(x, norm1_g, w_in, sb_norm_g, hg_lb_param, hg_norm_g, rw_mu, rw_w0, rw_w_w2, rw_a0,
           rw_w_a2, rw_w_g2, rw_k_k, rw_k_a, rw_r_k, rw_lnx_w, rw_lnx_b, w_out, norm2_g,
           w_ff_in, w_ff_out, final_g):
    b, t, d = x.shape
    depth = norm1_g.shape[0]
    sb_w = sb_norm_g.shape[1]
    hg_w = hg_lb_param.shape[1]
    rw_w = rw_lnx_w.shape[1]
    sb_cols, hg_cols, rkv_cols = 3 * sb_w, 4 * hg_w, 3 * rw_w
    m = b * t
    xf = x.reshape(m, d)
    w_in_l = w_in[:1].astype(BF16)
    h, ss = rmsnorm(xf, norm1_g[0], BF16), None
    for l in range(depth):
        proj = functools.partial(matmul, h, w_in_l, bm=1024, row_ss=ss)
        p_sb = proj(bn=1024, col0=0, n=sb_cols, out_dtype=BF16, name="proj_sb").reshape(b, t, -1)
        p_hg, w_out_l = proj(bn=1024, col0=sb_cols, n=hg_cols, side_cast=(w_out, l), name="proj_hg")
        p_hg = p_hg.reshape(b, t, -1)
        p_rkv = proj(bn=1024, col0=sb_cols + hg_cols, n=rkv_cols, name="proj_rkv").reshape(b, t, -1)
        p_lora = proj(bn=RW_LORA_COLS, col0=sb_cols + hg_cols + rkv_cols, n=RW_LORA_COLS,
                      name="proj_lora").reshape(b, t, -1)
        o_sb = sb_attention(p_sb, sb_norm_g[l])
        o_hg = hgrn2(p_hg, hg_lb_param, hg_norm_g[l], layer=l)
        o_rw = rwkv7(p_rkv, p_lora, rw_mu[l], rw_w0[l], rw_w_w2[l], rw_a0[l], rw_w_a2[l],
                     rw_w_g2[l], rw_k_k[l], rw_k_a[l], rw_r_k[l], rw_lnx_w[l], rw_lnx_b[l])
        mix = [o.reshape(m, -1) for o in (o_sb, o_hg, o_rw)]
        xf, h, ss, w_ff_in_l = matmul(mix, w_out_l, bm=1024, bn=512, residual=xf,
                                      next_norm_g=norm2_g[l], side_cast=(w_ff_in, l), name="out_proj")
        ff, w_ff_out_l = matmul(h, w_ff_in_l, bm=1024, bn=1024, relu2=True, row_ss=ss, out_dtype=BF16,
                                side_cast=(w_ff_out, l), name="ff_in")
        ff_out = functools.partial(matmul, ff, w_ff_out_l, bm=1024, bn=1024, bk=2048, residual=xf)
        if l + 1 < depth:
            xf, h, ss, w_in_l = ff_out(next_norm_g=norm1_g[l + 1], side_cast=(w_in, l + 1),
                                       name="ff_out_norm")
        else:
            xf = ff_out(name="ff_out")
    return rmsnorm(xf, final_g, F32).reshape(b, t, d)
```

```python
import functools
import math

import jax
import jax.numpy as jnp
import numpy as np
from jax import lax
from jax.experimental import pallas as pl
from jax.experimental.pallas import tpu as pltpu

F32 = jnp.float32
BF16 = jnp.bfloat16

NORM_EPS = 1e-5
V7X_VMEM_BYTES = 64 * 1024 * 1024
VMEM_LIMIT = 56 * 1024 * 1024
LANES = 128
BF16_SUBLANES = 16

SB_HEAD_DIM = 128
SB_LOG_WEIGHT_CUTOFF = -90.0
HG_HEAD_DIM = 128
HG_MIN_F = 1e-30
RW_HEAD_DIM = 64
RW_GN_EPS = 64e-5
RW_DECAY_RANK = 96
RW_AAA_RANK = 96
RW_GATE_RANK = 64
RW_LORA_COLS = RW_DECAY_RANK + RW_AAA_RANK + RW_GATE_RANK


def _cparams(*sem):
    return pltpu.CompilerParams(dimension_semantics=sem, vmem_limit_bytes=VMEM_LIMIT)


def _dot(a, b):
    return jnp.dot(a, b, preferred_element_type=F32)


def _dot_nt(a, b):
    return lax.dot_general(a, b, (((1,), (1,)), ((), ())), preferred_element_type=F32)


def _dot_tn(a, b):
    return lax.dot_general(a, b, (((0,), (0,)), ((), ())), preferred_element_type=F32)


def _split2(x):
    hi = x.astype(BF16)
    lo = (x - hi.astype(F32)).astype(BF16)
    return hi, lo


def _split3(x):
    hi = x.astype(BF16)
    r1 = x - hi.astype(F32)
    mid = r1.astype(BF16)
    lo = (r1 - mid.astype(F32)).astype(BF16)
    return hi, mid, lo


def _dot_exact_rhs01(a01, x, parts=3):
    ps = _split3(x) if parts == 3 else _split2(x)
    out = _dot(a01, ps[0])
    for p in ps[1:]:
        out = out + _dot(a01, p)
    return out


def _dot_exact_lhs(x, b01, parts=2):
    ps = _split3(x) if parts == 3 else _split2(x)
    out = _dot(ps[0], b01)
    for p in ps[1:]:
        out = out + _dot(p, b01)
    return out


def _rmsnorm_kernel(x_ref, g_ref, o_ref):
    x = x_ref[...]
    ms = jnp.mean(x * x, axis=-1, keepdims=True)
    o_ref[...] = (x * lax.rsqrt(ms + NORM_EPS) * g_ref[...]).astype(o_ref.dtype)


def rmsnorm(x, g, out_dtype, bm=512):
    m, d = x.shape
    bm = min(bm, m)
    return pl.pallas_call(
        _rmsnorm_kernel,
        out_shape=jax.ShapeDtypeStruct((m, d), out_dtype),
        grid=(m // bm,),
        in_specs=[pl.BlockSpec((bm, d), lambda i: (i, 0)),
                  pl.BlockSpec((1, d), lambda i: (0, 0))],
        out_specs=pl.BlockSpec((bm, d), lambda i: (i, 0)),
        compiler_params=_cparams("parallel"),
        name="rmsnorm",
    )(x, g.reshape(1, d).astype(F32))


def _matmul_kernel(*refs, n_lhs, nk, relu2, has_res, has_row_ss, emit_norm, has_side, norm_dim):
    refs = list(refs)
    a_refs = [refs.pop(0) for _ in range(n_lhs)]
    b_ref = refs.pop(0)
    r_ref = refs.pop(0) if has_res else None
    ss_in_ref = refs.pop(0) if has_row_ss else None
    gnext_ref = refs.pop(0) if emit_norm else None
    side_in_ref = refs.pop(0) if has_side else None
    o_ref = refs.pop(0)
    xg_ref, ss_out_ref = (refs.pop(0), refs.pop(0)) if emit_norm else (None, None)
    side_out_ref = refs.pop(0) if has_side else None
    rest = refs
    j = pl.program_id(1)

    if has_side:
        side_out_ref[...] = side_in_ref[...].astype(side_out_ref.dtype)

    def product():
        acc, k0 = None, 0
        for a_ref in a_refs:
            kw = a_ref.shape[1]
            part = _dot(a_ref[...], b_ref[k0:k0 + kw, :])
            acc = part if acc is None else acc + part
            k0 += kw
        return acc

    def finish(acc):
        if has_row_ss:
            acc = acc * lax.rsqrt(ss_in_ref[:, :1] * (1.0 / norm_dim) + NORM_EPS)
        if relu2:
            acc = jnp.square(jnp.maximum(acc, 0.0))
        if has_res:
            acc = acc + r_ref[...]
        o_ref[...] = acc.astype(o_ref.dtype)
        if emit_norm:
            xg_ref[...] = (acc * gnext_ref[...]).astype(xg_ref.dtype)
            part = jnp.broadcast_to(jnp.sum(acc * acc, axis=-1, keepdims=True), ss_out_ref.shape)

            @pl.when(j == 0)
            def _():
                ss_out_ref[...] = part

            @pl.when(j > 0)
            def _():
                ss_out_ref[...] += part

    if nk == 1:
        finish(product())
        return

    acc_ref = rest[0]
    k = pl.program_id(2)

    @pl.when(k == 0)
    def _():
        acc_ref[...] = product()

    @pl.when((k > 0) & (k < nk - 1))
    def _():
        acc_ref[...] += product()

    @pl.when(k == nk - 1)
    def _():
        finish(acc_ref[...] + product())


def matmul(a, b, *, bm, bn, bk=None, layer=0, col0=0, n=None, relu2=False, residual=None,
           row_ss=None, next_norm_g=None, side_cast=None, out_dtype=F32, name="matmul"):
    a_list = list(a) if isinstance(a, (list, tuple)) else [a]
    m = a_list[0].shape[0]
    kdim = sum(x.shape[1] for x in a_list)
    b_rows, b_cols = b.shape[-2:]
    n = b_cols - col0 if n is None else n
    bm, bn = min(bm, m), min(bn, n)
    bk = kdim if bk is None else min(bk, kdim)
    assert b_rows == kdim and m % bm == 0 and n % bn == 0 and kdim % bk == 0 and col0 % bn == 0
    nk = kdim // bk
    assert nk == 1 or len(a_list) == 1
    has_res = residual is not None
    jb0 = col0 // bn
    if len(a_list) == 1:
        in_specs = [pl.BlockSpec((bm, bk), lambda i, j, k: (i, k))]
    else:
        in_specs = [pl.BlockSpec((bm, x.shape[1]), lambda i, j, k: (i, 0)) for x in a_list]
    if b.ndim == 3:
        in_specs.append(pl.BlockSpec((None, bk, bn), lambda i, j, k: (layer, k, jb0 + j)))
    else:
        in_specs.append(pl.BlockSpec((bk, bn), lambda i, j, k: (k, jb0 + j)))
    args = a_list + [b]
    if has_res:
        in_specs.append(pl.BlockSpec((bm, bn), lambda i, j, k: (i, j)))
        args.append(residual)
    stat_spec = pl.BlockSpec((bm, LANES), lambda i, j, k: (i, 0))
    if row_ss is not None:
        in_specs.append(stat_spec)
        args.append(row_ss)
    emit_norm = next_norm_g is not None
    out_shape = jax.ShapeDtypeStruct((m, n), out_dtype)
    out_specs = pl.BlockSpec((bm, bn), lambda i, j, k: (i, j))
    if emit_norm:
        assert col0 == 0 and n == b_cols
        in_specs.append(pl.BlockSpec((1, bn), lambda i, j, k: (0, j)))
        args.append(next_norm_g.reshape(1, n).astype(F32))
        out_shape = [out_shape, jax.ShapeDtypeStruct((m, n), BF16),
                     jax.ShapeDtypeStruct((m, LANES), F32)]
        out_specs = [out_specs, pl.BlockSpec((bm, bn), lambda i, j, k: (i, j)), stat_spec]
    else:
        out_shape, out_specs = [out_shape], [out_specs]
    grid = (m // bm, n // bn, nk)
    if side_cast is not None:
        src, src_layer = side_cast
        _, src_rows, src_cols = src.shape
        steps = grid[0] * grid[1] * grid[2]
        side_rows = max(src_rows // steps, BF16_SUBLANES)
        n_slices = src_rows // side_rows
        assert src_rows == n_slices * side_rows and steps % n_slices == 0
        hold = steps // n_slices

        def side_block(i, j, k):
            return ((i * grid[1] + j) * grid[2] + k) // hold

        in_specs.append(pl.BlockSpec((None, side_rows, src_cols),
                                     lambda i, j, k: (src_layer, side_block(i, j, k), 0)))
        args.append(src)
        out_shape.append(jax.ShapeDtypeStruct((src_rows, src_cols), BF16))
        out_specs.append(pl.BlockSpec((side_rows, src_cols), lambda i, j, k: (side_block(i, j, k), 0)))
    outs = pl.pallas_call(
        functools.partial(_matmul_kernel, n_lhs=len(a_list), nk=nk, relu2=relu2, has_res=has_res,
                          has_row_ss=row_ss is not None, emit_norm=emit_norm,
                          has_side=side_cast is not None, norm_dim=kdim),
        out_shape=out_shape,
        grid=grid,
        in_specs=in_specs,
        out_specs=out_specs,
        scratch_shapes=[pltpu.VMEM((bm, bn), F32)] if nk > 1 else [],
        compiler_params=_cparams("parallel", "arbitrary" if emit_norm else "parallel", "arbitrary"),
        name=name,
    )(*args)
    return outs[0] if len(outs) == 1 else tuple(outs)


def _sb_kernel(q_ref, k_ref, v_ref, g_ref, o_ref, *, blk, heads):
    i = pl.program_id(2)
    scale = SB_HEAD_DIM ** -0.5
    hd = SB_HEAD_DIM
    qs = [q_ref[0, :, h * hd:(h + 1) * hd].astype(BF16) for h in range(heads)]
    row = lax.broadcasted_iota(jnp.int32, (blk, blk), 0)
    col = lax.broadcasted_iota(jnp.int32, (blk, blk), 1)
    before = col < row
    r2 = lax.broadcasted_iota(jnp.int32, (blk, 2 * blk), 0)
    c2 = lax.broadcasted_iota(jnp.int32, (blk, 2 * blk), 1)
    later = jnp.where((r2 > c2) | (c2 >= blk), 1.0, 0.0).astype(BF16)

    def blocks(specs, carry):
        hs = range(heads)
        units = [(s, h) for s in range(len(specs)) for h in hs]
        starts = [pl.multiple_of(j * blk, blk) for j, _ in specs]
        kbs = [k_ref[0, pl.ds(starts[s], blk), h * hd:(h + 1) * hd].astype(BF16) for s, h in units]
        vbs = [v_ref[0, pl.ds(starts[s], blk), h * hd:(h + 1) * hd].astype(BF16) for s, h in units]
        zs = [_dot_nt(qs[h], kbs[u]) * scale for u, (s, h) in enumerate(units)]
        log_betas = [jnp.minimum(z, 0.0) - jnp.log(1.0 + jnp.exp(-jnp.abs(z))) for z in zs]
        log_keeps = [lb - z for lb, z in zip(log_betas, zs)]
        log_keeps = [lk if specs[s][1] is None else jnp.where(specs[s][1], lk, 0.0)
                     for lk, (s, h) in zip(log_keeps, units)]
        cs_all = _dot_exact_lhs(jnp.concatenate(log_keeps, axis=0), later, parts=2)
        css = [cs_all[u * blk:(u + 1) * blk] for u in range(len(units))]
        accs = [carry[2 * h] for h in hs]
        cums = [carry[2 * h + 1] for h in hs]
        for u, (s, h) in enumerate(units):
            w = jnp.exp(log_betas[u] + cums[h] + css[u][:, :blk])
            if specs[s][1] is not None:
                w = jnp.where(specs[s][1], w, 0.0)
            accs[h] = accs[h] + _dot(w.astype(BF16), vbs[u])
            cums[h] = cums[h] + css[u][:, blk:]
        out = []
        for h in hs:
            out += [accs[h], cums[h]]
        return tuple(out)

    carry = (jnp.zeros((blk, hd), F32), jnp.zeros((blk, blk), F32)) * heads
    has_previous = (jnp.zeros((blk, blk), jnp.int32) + i) > 0
    carry = blocks([(i, before), (jnp.maximum(i - 1, 0), has_previous)], carry)

    def alive(state):
        j = state[0]
        top = state[2]
        for h in range(1, heads):
            top = jnp.maximum(top, state[2 + 2 * h])
        return (j >= 0) & (jnp.max(top) > SB_LOG_WEIGHT_CUTOFF)

    def body(state):
        j = state[0]
        return (j - 1,) + blocks([(j, None)], state[1:])

    state = lax.while_loop(alive, body, (i - 2,) + carry)
    for h in range(heads):
        acc = state[1 + 2 * h]
        ms = jnp.mean(acc * acc, axis=-1, keepdims=True)
        o_ref[0, :, h * hd:(h + 1) * hd] = (
            acc * lax.rsqrt(ms + NORM_EPS) * g_ref[:, h * hd:(h + 1) * hd]).astype(o_ref.dtype)


def sb_attention(proj_sb, norm_g, *, blk=128, heads=8):
    b, t, w3 = proj_sb.shape
    w = w3 // 3
    bw = heads * SB_HEAD_DIM
    nh = w // bw
    blk = min(blk, t)
    return pl.pallas_call(
        functools.partial(_sb_kernel, blk=blk, heads=heads),
        out_shape=jax.ShapeDtypeStruct((b, t, w), BF16),
        grid=(b, nh, t // blk),
        in_specs=[pl.BlockSpec((1, blk, bw), lambda bi, h, i: (bi, i, h)),
                  pl.BlockSpec((1, t, bw), lambda bi, h, i: (bi, 0, nh + h)),
                  pl.BlockSpec((1, t, bw), lambda bi, h, i: (bi, 0, 2 * nh + h)),
                  pl.BlockSpec((1, bw), lambda bi, h, i: (0, h))],
        out_specs=pl.BlockSpec((1, blk, bw), lambda bi, h, i: (bi, i, h)),
        compiler_params=_cparams("parallel", "parallel", "arbitrary"),
        name="sb_attention",
    )(proj_sb, proj_sb, proj_sb, norm_g.reshape(1, w).astype(F32))


def _hg_level_halves(chunk):
    halves = []
    m = chunk // 2
    while m >= 1:
        halves.append(m)
        m //= 2
    return halves


HG_ROW_TILE = 8


def _hg_decay_selectors(chunk):
    t = lax.broadcasted_iota(jnp.int32, (chunk, chunk), 0)
    j = lax.broadcasted_iota(jnp.int32, (chunk, chunk), 1)
    mats = [j <= t]
    for m in _hg_level_halves(chunk):
        if m >= HG_ROW_TILE:
            continue
        mid = (t // (2 * m)) * (2 * m) + m - 1
        upper = (t % (2 * m)) >= m
        mats.append((upper & (j > mid) & (j <= t)) | (~upper & (j > t) & (j <= mid)))
    return jnp.concatenate([jnp.where(mm, 1.0, 0.0).astype(BF16) for mm in mats], axis=0)


def _hg_level_decays(sums, chunk):
    b = sums[0:chunk]
    out, small = [], 0
    for m in _hg_level_halves(chunk):
        if m >= HG_ROW_TILE:
            mids = [jnp.broadcast_to(b[s + m - 1:s + m, :], (2 * m, b.shape[1]))
                    for s in range(0, chunk, 2 * m)]
            b_mid = mids[0] if len(mids) == 1 else jnp.concatenate(mids, axis=0)
            out.append(jnp.exp(-jnp.abs(b - b_mid)))
        else:
            small += 1
            out.append(jnp.exp(sums[small * chunk:(small + 1) * chunk]))
    return out


def _hg_kernel(q_ref, f_ref, i_ref, g_ref, lbp_ref, ng_ref, o_ref, st_ref, *, chunk, layer, heads):
    hd = HG_HEAD_DIM
    tb = q_ref.shape[1]

    @pl.when(pl.program_id(2) == 0)
    def _():
        st_ref[...] = jnp.zeros_like(st_ref)

    halves = _hg_level_halves(chunk)
    sel = _hg_decay_selectors(chunk)

    p = lbp_ref[...]
    e = jnp.exp(p - jnp.max(p, axis=0, keepdims=True))
    probs = e / jnp.sum(e, axis=0, keepdims=True)
    lb = jnp.zeros((1, heads * hd), F32)
    for l in range(1, layer + 1):
        lb = lb + probs[l:l + 1, :]

    trow = lax.broadcasted_iota(jnp.int32, (chunk, heads * hd), 0)
    ts = lax.broadcasted_iota(jnp.int32, (chunk, chunk), 0)
    ss = lax.broadcasted_iota(jnp.int32, (chunk, chunk), 1)
    ng = ng_ref[...]

    def step(c, _):
        start = pl.multiple_of(c * chunk, chunk)
        rows = pl.ds(start, chunk)
        q2 = q_ref[0, rows, :]
        fr = f_ref[0, rows, :]
        v2 = i_ref[0, rows, :]
        gate = g_ref[0, rows, :]
        f = lb + (1.0 - lb) * jax.nn.sigmoid(fr)
        log_f = jnp.log(jnp.maximum(f, HG_MIN_F))
        key2 = (1.0 - lb) * jax.nn.sigmoid(-fr)
        sums = _dot_exact_rhs01(sel, log_f, parts=2)
        b = sums[0:chunk]
        b_last = b[chunk - 1:chunk, :]
        level_decay = _hg_level_decays(sums, chunk)
        silu = gate * jax.nn.sigmoid(gate)

        hs = range(heads)
        lanes = [slice(h * hd, (h + 1) * hd) for h in hs]
        states = [st_ref[h] for h in hs]
        vbs = [v2[:, l].astype(BF16) for l in lanes]
        scores = [None] * heads
        for li, m in enumerate(halves):
            el = level_decay[li]
            upper = (trow % (2 * m)) >= m
            ql = jnp.where(upper, q2 * el, 0.0).astype(BF16)
            kl = jnp.where(upper, 0.0, key2 * el).astype(BF16)
            for h in hs:
                sl = _dot_nt(ql[:, lanes[h]], kl[:, lanes[h]])
                if 2 * m < chunk:
                    sl = jnp.where((ts // (2 * m)) == (ss // (2 * m)), sl, 0.0)
                scores[h] = sl if scores[h] is None else scores[h] + sl
        q_pre = (q2 * jnp.exp(b)).astype(BF16)
        k_suf = (key2 * jnp.exp(b_last - b)).astype(BF16)
        qk = q2 * key2
        inter = [_dot_nt(q_pre[:, lanes[h]], states[h].astype(BF16)) for h in hs]
        intra = [_dot(scores[h].astype(BF16), vbs[h]) for h in hs]
        upd = [_dot_tn(vbs[h], k_suf[:, lanes[h]]) for h in hs]
        decay_all = jnp.exp(b_last)
        for h in hs:
            l = lanes[h]
            st_ref[h] = states[h] * decay_all[:, l] + upd[h]
            diag = jnp.sum(qk[:, l], axis=-1, keepdims=True)
            o = intra[h] + diag * v2[:, l] + inter[h]
            ms = jnp.mean(o * o, axis=-1, keepdims=True)
            o = o * lax.rsqrt(ms + NORM_EPS) * ng
            o_ref[0, rows, l] = (o * silu[:, l]).astype(o_ref.dtype)
        return 0

    lax.fori_loop(0, tb // chunk, step, 0, unroll=2)


def hgrn2(proj_hg, lb_param, norm_g, *, layer, chunk=128, heads=4, tb=512):
    b, t, w4 = proj_hg.shape
    w = w4 // 4
    bw = heads * HG_HEAD_DIM
    nh = w // bw
    tb = min(tb, t)
    chunk = min(chunk, tb)
    depth = lb_param.shape[0]

    def col(g):
        return pl.BlockSpec((1, tb, bw), lambda bi, h, ti: (bi, ti, g * nh + h))

    return pl.pallas_call(
        functools.partial(_hg_kernel, chunk=chunk, layer=layer, heads=heads),
        out_shape=jax.ShapeDtypeStruct((b, t, w), BF16),
        grid=(b, nh, t // tb),
        in_specs=[col(0), col(1), col(2), col(3),
                  pl.BlockSpec((depth, bw), lambda bi, h, ti: (0, h)),
                  pl.BlockSpec((1, HG_HEAD_DIM), lambda bi, h, ti: (0, 0))],
        out_specs=pl.BlockSpec((1, tb, bw), lambda bi, h, ti: (bi, ti, h)),
        scratch_shapes=[pltpu.VMEM((heads, HG_HEAD_DIM, HG_HEAD_DIM), F32)],
        compiler_params=_cparams("parallel", "parallel", "arbitrary"),
        name="hgrn2",
    )(proj_hg, proj_hg, proj_hg, proj_hg, lb_param.astype(F32),
      norm_g.reshape(1, HG_HEAD_DIM).astype(F32))


RW_GROUP_HEADS = 4
RW_GROUP = RW_GROUP_HEADS * RW_HEAD_DIM


def _head_ones(n, head):
    r = lax.broadcasted_iota(jnp.int32, (n, n), 0)
    c = lax.broadcasted_iota(jnp.int32, (n, n), 1)
    return jnp.where((r // head) == (c // head), 1.0, 0.0).astype(BF16)


def _head_sums(x, ones_bd):
    n = ones_bd.shape[0]
    rows, width = x.shape
    if width == n:
        return _dot_exact_lhs(x, ones_bd, parts=2)
    stacked = jnp.concatenate([x[:, s:s + n] for s in range(0, width, n)], axis=0)
    sums = _dot_exact_lhs(stacked, ones_bd, parts=2)
    return jnp.concatenate([sums[g * rows:(g + 1) * rows] for g in range(width // n)], axis=1)


def _softplus(y):
    return jnp.maximum(y, 0.0) + jnp.log(1.0 + jnp.exp(-jnp.abs(y)))


def _rw_prepare_block(i, x_ref, xp_ref, lo_ref, lop_ref, mu_ref, mulo_ref, w0_ref, a0_ref, kk_ref,
                      ka_ref, rk_ref, w2w_ref, w2a_ref, w2g_ref,
                      r_out, lw_out, k_out, v_out, kk_out, kb_out, g_out, bonus_out):
    width = r_out.shape[1]

    def token_shift(cur_ref, prev_ref, mix_ref):
        x = cur_ref[0]
        prev_row = jnp.where(i == 0, 0.0, prev_ref[0, 7:8, :])
        row = lax.broadcasted_iota(jnp.int32, x.shape, 0)
        prev = jnp.where(row == 0, prev_row, pltpu.roll(x, 1, axis=0))
        return x + mix_ref[...] * (prev - x)

    xs = token_shift(x_ref, xp_ref, mu_ref)
    lora = token_shift(lo_ref, lop_ref, mulo_ref)
    w_log = -_softplus(-(w0_ref[...] + _dot(jnp.tanh(lora).astype(BF16), w2w_ref[...]))) - 0.5
    log_decay = -jnp.exp(w_log)
    a = jax.nn.sigmoid(a0_ref[...] + _dot(lora.astype(BF16), w2a_ref[...]))
    g = _dot(jax.nn.sigmoid(lora).astype(BF16), w2g_ref[...])

    r = xs[:, :width]
    k = xs[:, width:2 * width]
    v = xs[:, 2 * width:3 * width]
    ones_bd = _head_ones(RW_GROUP, RW_HEAD_DIM)
    kk = k * kk_ref[...]
    kk = kk * lax.rsqrt(jnp.maximum(_head_sums(kk * kk, ones_bd), 1e-24))
    k = k * (1.0 + (a - 1.0) * ka_ref[...])
    bonus = _head_sums(r * k * rk_ref[...], ones_bd) * v

    r_out[...] = r
    lw_out[...] = log_decay
    k_out[...] = k
    v_out[...] = v
    kk_out[...] = kk
    kb_out[...] = kk * a
    g_out[...] = g
    bonus_out[...] = bonus


def _tile_rows(x, n):
    return jnp.concatenate([x] * n, axis=0)


def _rw_kernel(x_ref, xp_ref, lo_ref, lop_ref, mu_ref, mulo_ref, w0_ref, a0_ref, kkp_ref, ka_ref,
               rk_ref, w2w_ref, w2a_ref, w2g_ref, lnw_ref, lnb_ref, o_ref,
               ht_ref, r_ref, lw_ref, k_ref, v_ref, kk_ref, kb_ref, g_ref, bonus_ref,
               *, chunk, inv_parts):
    nh, hd, gw = RW_GROUP_HEADS, RW_HEAD_DIM, RW_GROUP
    wide = nh * chunk
    tb, width = r_ref.shape
    groups = width // gw
    ti = pl.program_id(1)

    @pl.when(ti == 0)
    def _():
        ht_ref[...] = jnp.zeros_like(ht_ref)

    _rw_prepare_block(ti, x_ref, xp_ref, lo_ref, lop_ref, mu_ref, mulo_ref, w0_ref, a0_ref, kkp_ref,
                      ka_ref, rk_ref, w2w_ref, w2a_ref, w2g_ref,
                      r_ref, lw_ref, k_ref, v_ref, kk_ref, kb_ref, g_ref, bonus_ref)

    def iota(shape, d):
        return lax.broadcasted_iota(jnp.int32, shape, d)

    ltri = jnp.where(iota((chunk, chunk), 1) <= iota((chunk, chunk), 0), 1.0, 0.0).astype(BF16)
    row_cw = iota((chunk, wide), 0)
    pos_cw = iota((chunk, wide), 1) % chunk
    strict = pos_cw < row_cw
    incl = pos_cw <= row_cw
    eye_t = jnp.where(pos_cw == row_cw, 1.0, 0.0)
    off_diag = []
    s = 1
    while s < chunk:
        off_diag.append(((row_cw // (2 * s)) == (pos_cw // (2 * s)))
                        & ((row_cw % (2 * s)) >= s) & ((pos_cw % (2 * s)) < s))
        s *= 2
    bd_wk = (iota((wide, gw), 0) // chunk) == (iota((wide, gw), 1) // hd)
    bd_ww = (iota((wide, wide), 0) // chunk) == (iota((wide, wide), 1) // chunk)
    head_mask = (iota((gw, gw), 0) // hd) == (iota((gw, gw), 1) // hd)
    ones_bd = _head_ones(gw, hd)

    def expand_k(x):
        return jnp.where(bd_wk, _tile_rows(x, nh), 0.0).astype(BF16)

    def expand_w(x):
        return jnp.where(bd_ww, _tile_rows(x, nh), 0.0)

    def mm(a, b):
        if inv_parts == 1:
            return _dot(a.astype(BF16), b.astype(BF16))
        ah, al = _split2(a)
        bh, bl = _split2(b)
        return _dot(ah, bh) + (_dot(ah, bl) + _dot(al, bh))

    def step(c, _):
        rows = pl.ds(pl.multiple_of(c * chunk, chunk), chunk)
        gs = range(groups)
        lanes = [slice(gi * gw, (gi + 1) * gw) for gi in gs]
        lw = lw_ref[rows, :]
        gam = _dot_exact_rhs01(ltri, lw, parts=2)
        gam_end = gam[chunk - 1:chunk, :]
        e_neg = jnp.exp(-gam)
        e_suf = jnp.exp(gam_end - gam)
        k_all = k_ref[rows, :]
        kb_all = kb_ref[rows, :]
        v_all = v_ref[rows, :]
        a_t = -kk_ref[rows, :] * jnp.exp(gam - lw)
        r_t = r_ref[rows, :] * jnp.exp(gam)
        b_t = kb_all * e_neg
        k_t = k_all * e_neg
        b_h = kb_all * e_suf
        k_h = k_all * e_suf
        h_decay = jnp.exp(gam_end)

        lhs = [jnp.concatenate([a_t[:, l], r_t[:, l]], axis=0).astype(BF16) for l in lanes]
        sb = [_dot_nt(lhs[g], expand_k(b_t[:, lanes[g]])) for g in gs]
        sk = [_dot_nt(lhs[g], expand_k(k_t[:, lanes[g]])) for g in gs]
        a_ab = [jnp.where(strict, s[:chunk], 0.0) for s in sb]
        a_rb = [jnp.where(incl, s[chunk:], 0.0) for s in sb]
        a_ak = [jnp.where(strict, s[:chunk], 0.0) for s in sk]
        a_rk = [jnp.where(incl, s[chunk:], 0.0) for s in sk]

        hts = [ht_ref[g] for g in gs]
        from_state = [_dot_nt(lhs[g], hts[g].astype(BF16)) for g in gs]
        from_v = [_dot(jnp.concatenate([a_ak[g], a_rk[g]], axis=0).astype(BF16),
                       expand_k(v_all[:, lanes[g]])) for g in gs]

        p_acc = [eye_t + jnp.where(off_diag[0], a, 0.0) for a in a_ab]
        for li in range(1, len(off_diag)):
            left = [mm(p_acc[g], expand_w(jnp.where(off_diag[li], a_ab[g], 0.0))) for g in gs]
            p_acc = [p_acc[g] + mm(left[g], expand_w(p_acc[g])) for g in gs]

        u = [_dot(p_acc[g].astype(BF16), expand_k(from_state[g][:chunk] + from_v[g][:chunk]))
             for g in gs]
        y = [from_state[g][chunk:] + from_v[g][chunk:] + _dot(a_rb[g].astype(BF16), expand_k(u[g]))
             for g in gs]
        upd = [_dot_tn(jnp.concatenate([u[g], v_all[:, lanes[g]]], axis=0).astype(BF16),
                       jnp.concatenate([b_h[:, lanes[g]], k_h[:, lanes[g]]], axis=0).astype(BF16))
               for g in gs]
        for g in gs:
            ht_ref[g] = hts[g] * h_decay[:, lanes[g]] + jnp.where(head_mask, upd[g], 0.0)

        y_all = jnp.concatenate(y, axis=0)
        yc_all = y_all - _head_sums(y_all, ones_bd) * (1.0 / hd)
        var_all = _head_sums(yc_all * yc_all, ones_bd) * (1.0 / hd)
        yn_all = yc_all * lax.rsqrt(var_all + RW_GN_EPS)
        for g in gs:
            l = lanes[g]
            yn = yn_all[g * chunk:(g + 1) * chunk] * lnw_ref[:, l] + lnb_ref[:, l]
            out = (yn + bonus_ref[rows, l]) * g_ref[rows, l]
            o_ref[0, rows, l] = out.astype(o_ref.dtype)
        return 0

    lax.fori_loop(0, tb // chunk, step, 0, unroll=2)


def rwkv7(rkv, lora, mu, w0, w_w2, a0, w_a2, w_g2, k_k, k_a, r_k, lnx_w, lnx_b, *, chunk=64, tb=256,
          inv_parts=1):
    b, t, cols = rkv.shape
    width = cols // 3
    tb = min(tb, t)

    def padded(w2, first_row):
        full = jnp.zeros((RW_LORA_COLS, width), F32)
        return lax.dynamic_update_slice(full, w2.astype(F32), (first_row, 0)).astype(BF16)

    w2w = padded(w_w2, 0)
    w2a = padded(w_a2, RW_DECAY_RANK)
    w2g = padded(w_g2, RW_DECAY_RANK + RW_AAA_RANK)

    def vec(p):
        return p.reshape(1, -1).astype(F32)

    def vspec(n):
        return pl.BlockSpec((1, n), lambda bi, i: (0, 0))

    def cur(n):
        return pl.BlockSpec((1, tb, n), lambda bi, i: (bi, i, 0))

    def prev(n):
        return pl.BlockSpec((1, 8, n), lambda bi, i: (bi, jnp.maximum(i * (tb // 8) - 1, 0), 0))

    wspec = pl.BlockSpec((RW_LORA_COLS, width), lambda bi, i: (0, 0))
    return pl.pallas_call(
        functools.partial(_rw_kernel, chunk=chunk, inv_parts=inv_parts),
        out_shape=jax.ShapeDtypeStruct((b, t, width), BF16),
        grid=(b, t // tb),
        in_specs=[cur(cols), prev(cols), cur(RW_LORA_COLS), prev(RW_LORA_COLS),
                  vspec(cols), vspec(RW_LORA_COLS),
                  vspec(width), vspec(width), vspec(width), vspec(width), vspec(width),
                  wspec, wspec, wspec, vspec(width), vspec(width)],
        out_specs=cur(width),
        scratch_shapes=[pltpu.VMEM((width // RW_GROUP, RW_GROUP, RW_GROUP), F32)]
                       + [pltpu.VMEM((tb, width), F32)] * 8,
        compiler_params=_cparams("parallel", "arbitrary"),
        name="rwkv7",
    )(rkv, rkv, lora, lora, vec(mu[:cols]), vec(mu[cols:]), vec(w0), vec(a0), vec(k_k), vec(k_a),
      vec(r_k), w2w, w2a, w2g, vec(lnx_w), vec(lnx_b))


def kernel(x, norm1_g, w_in, sb_norm_g, hg_lb_param, hg_norm_g, rw_mu, rw_w0, rw_w_w2, rw_a0,
           rw_w_a2, rw_w_g2, rw_k_k, rw_k_a, rw_r_k, rw_lnx_w, rw_lnx_b, w_out, norm2_g,
           w_ff_in, w_ff_out, final_g):
    b, t, d = x.shape
    depth = norm1_g.shape[0]
    sb_w = sb_norm_g.shape[1]
    hg_w = hg_lb_param.shape[1]
    rw_w = rw_lnx_w.shape[1]
    sb_cols, hg_cols, rkv_cols = 3 * sb_w, 4 * hg_w, 3 * rw_w
    m = b * t
    xf = x.reshape(m, d)
    w_in_l = w_in[:1].astype(BF16)
    h, ss = rmsnorm(xf, norm1_g[0], BF16), None
    for l in range(depth):
        proj = functools.partial(matmul, h, w_in_l, bm=1024, row_ss=ss)
        p_sb = proj(bn=1024, col0=0, n=sb_cols, out_dtype=BF16, name="proj_sb").reshape(b, t, -1)
        p_hg, w_out_l = proj(bn=1024, col0=sb_cols, n=hg_cols, side_cast=(w_out, l), name="proj_hg")
        p_hg = p_hg.reshape(b, t, -1)
        p_rkv = proj(bn=1024, col0=sb_cols + hg_cols, n=rkv_cols, name="proj_rkv").reshape(b, t, -1)
        p_lora = proj(bn=RW_LORA_COLS, col0=sb_cols + hg_cols + rkv_cols, n=RW_LORA_COLS,
                      name="proj_lora").reshape(b, t, -1)
        o_sb = sb_attention(p_sb, sb_norm_g[l])
        o_hg = hgrn2(p_hg, hg_lb_param, hg_norm_g[l], layer=l)
        o_rw = rwkv7(p_rkv, p_lora, rw_mu[l], rw_w0[l], rw_w_w2[l], rw_a0[l], rw_w_a2[l],
                     rw_w_g2[l], rw_k_k[l], rw_k_a[l], rw_r_k[l], rw_lnx_w[l], rw_lnx_b[l])
        mix = [o.reshape(m, -1) for o in (o_sb, o_hg, o_rw)]
        xf, h, ss, w_ff_in_l = matmul(mix, w_out_l, bm=1024, bn=512, residual=xf,
                                      next_norm_g=norm2_g[l], side_cast=(w_ff_in, l), name="out_proj")
        ff, w_ff_out_l = matmul(h, w_ff_in_l, bm=1024, bn=1024, relu2=True, row_ss=ss, out_dtype=BF16,
                                side_cast=(w_ff_out, l), name="ff_in")
        ff_out = functools.partial(matmul, ff, w_ff_out_l, bm=1024, bn=1024, bk=2048, residual=xf)
        if l + 1 < depth:
            xf, h, ss, w_in_l = ff_out(next_norm_g=norm1_g[l + 1], side_cast=(w_in, l + 1),
                                       name="ff_out_norm")
        else:
            xf = ff_out(name="ff_out")
    return rmsnorm(xf, final_g, F32).reshape(b, t, d)
```

```python
import functools

import jax
import jax.numpy as jnp
from jax import lax
from jax.experimental import pallas as pl
from jax.experimental.pallas import tpu as pltpu

F32 = jnp.float32
BF16 = jnp.bfloat16

NORM_EPS = 1e-5
V7X_VMEM_BYTES = 64 * 1024 * 1024
VMEM_LIMIT = V7X_VMEM_BYTES - 8 * 1024 * 1024
LANES = 128
BF16_SUBLANES = 16

SB_HEAD_DIM = 128
SB_LOG_WEIGHT_CUTOFF = -90.0
HG_HEAD_DIM = 128
HG_MIN_F = 1e-30
RW_HEAD_DIM = 64
RW_GN_EPS = 64e-5
RW_DECAY_RANK = 96
RW_AAA_RANK = 96
RW_GATE_RANK = 64
RW_LORA_COLS = RW_DECAY_RANK + RW_AAA_RANK + RW_GATE_RANK


def _cparams(*sem):
    return pltpu.CompilerParams(dimension_semantics=sem, vmem_limit_bytes=VMEM_LIMIT)


def _dot(a, b):
    return jnp.dot(a, b, preferred_element_type=F32)


def _dot_nt(a, b):
    return lax.dot_general(a, b, (((1,), (1,)), ((), ())), preferred_element_type=F32)


def _dot_tn(a, b):
    return lax.dot_general(a, b, (((0,), (0,)), ((), ())), preferred_element_type=F32)


def _split2(x):
    hi = x.astype(BF16)
    lo = (x - hi.astype(F32)).astype(BF16)
    return hi, lo


def _dot_select_rows(a01, x):
    hi, lo = _split2(x)
    return _dot(a01, hi) + _dot(a01, lo)


def _dot_select_cols(x, b01):
    hi, lo = _split2(x)
    return _dot(hi, b01) + _dot(lo, b01)


def _rmsnorm_kernel(x_ref, g_ref, o_ref):
    x = x_ref[...]
    ms = jnp.mean(x * x, axis=-1, keepdims=True)
    o_ref[...] = (x * lax.rsqrt(ms + NORM_EPS) * g_ref[...]).astype(o_ref.dtype)


def rmsnorm(x, g, out_dtype, bm=512):
    m, d = x.shape
    bm = min(bm, m)
    return pl.pallas_call(
        _rmsnorm_kernel,
        out_shape=jax.ShapeDtypeStruct((m, d), out_dtype),
        grid=(m // bm,),
        in_specs=[pl.BlockSpec((bm, d), lambda i: (i, 0)),
                  pl.BlockSpec((1, d), lambda i: (0, 0))],
        out_specs=pl.BlockSpec((bm, d), lambda i: (i, 0)),
        compiler_params=_cparams("parallel"),
        name="rmsnorm",
    )(x, g.reshape(1, d).astype(F32))


def _matmul_kernel(*refs, n_lhs, nk, relu2, has_res, has_row_ss, emit_norm, has_side, norm_dim):
    refs = list(refs)
    a_refs = [refs.pop(0) for _ in range(n_lhs)]
    b_ref = refs.pop(0)
    r_ref = refs.pop(0) if has_res else None
    ss_in_ref = refs.pop(0) if has_row_ss else None
    gnext_ref = refs.pop(0) if emit_norm else None
    side_in_ref = refs.pop(0) if has_side else None
    o_ref = refs.pop(0)
    xg_ref, ss_out_ref = (refs.pop(0), refs.pop(0)) if emit_norm else (None, None)
    side_out_ref = refs.pop(0) if has_side else None
    rest = refs
    j = pl.program_id(1)

    if has_side:
        side_out_ref[...] = side_in_ref[...].astype(side_out_ref.dtype)

    def product():
        acc, k0 = None, 0
        for a_ref in a_refs:
            kw = a_ref.shape[1]
            part = _dot(a_ref[...], b_ref[k0:k0 + kw, :])
            acc = part if acc is None else acc + part
            k0 += kw
        return acc

    def finish(acc):
        if has_row_ss:
            acc = acc * lax.rsqrt(ss_in_ref[:, :1] * (1.0 / norm_dim) + NORM_EPS)
        if relu2:
            acc = jnp.square(jnp.maximum(acc, 0.0))
        if has_res:
            acc = acc + r_ref[...]
        o_ref[...] = acc.astype(o_ref.dtype)
        if emit_norm:
            xg_ref[...] = (acc * gnext_ref[...]).astype(xg_ref.dtype)
            part = jnp.broadcast_to(jnp.sum(acc * acc, axis=-1, keepdims=True), ss_out_ref.shape)

            @pl.when(j == 0)
            def _():
                ss_out_ref[...] = part

            @pl.when(j > 0)
            def _():
                ss_out_ref[...] += part

    if nk == 1:
        finish(product())
        return

    acc_ref = rest[0]
    k = pl.program_id(2)

    @pl.when(k == 0)
    def _():
        acc_ref[...] = product()

    @pl.when((k > 0) & (k < nk - 1))
    def _():
        acc_ref[...] += product()

    @pl.when(k == nk - 1)
    def _():
        finish(acc_ref[...] + product())


def matmul(a, b, *, bm, bn, bk=None, layer=0, col0=0, n=None, relu2=False, residual=None,
           row_ss=None, next_norm_g=None, side_cast=None, out_dtype=F32, name="matmul"):
    a_list = list(a) if isinstance(a, (list, tuple)) else [a]
    m = a_list[0].shape[0]
    kdim = sum(x.shape[1] for x in a_list)
    b_rows, b_cols = b.shape[-2:]
    n = b_cols - col0 if n is None else n
    bm, bn = min(bm, m), min(bn, n)
    bk = kdim if bk is None else min(bk, kdim)
    assert b_rows == kdim and m % bm == 0 and n % bn == 0 and kdim % bk == 0 and col0 % bn == 0
    nk = kdim // bk
    assert nk == 1 or len(a_list) == 1
    has_res = residual is not None
    jb0 = col0 // bn
    if len(a_list) == 1:
        in_specs = [pl.BlockSpec((bm, bk), lambda i, j, k: (i, k))]
    else:
        in_specs = [pl.BlockSpec((bm, x.shape[1]), lambda i, j, k: (i, 0)) for x in a_list]
    if b.ndim == 3:
        in_specs.append(pl.BlockSpec((None, bk, bn), lambda i, j, k: (layer, k, jb0 + j)))
    else:
        in_specs.append(pl.BlockSpec((bk, bn), lambda i, j, k: (k, jb0 + j)))
    args = a_list + [b]
    if has_res:
        in_specs.append(pl.BlockSpec((bm, bn), lambda i, j, k: (i, j)))
        args.append(residual)
    stat_spec = pl.BlockSpec((bm, LANES), lambda i, j, k: (i, 0))
    if row_ss is not None:
        in_specs.append(stat_spec)
        args.append(row_ss)
    emit_norm = next_norm_g is not None
    out_shape = jax.ShapeDtypeStruct((m, n), out_dtype)
    out_specs = pl.BlockSpec((bm, bn), lambda i, j, k: (i, j))
    if emit_norm:
        assert col0 == 0 and n == b_cols
        in_specs.append(pl.BlockSpec((1, bn), lambda i, j, k: (0, j)))
        args.append(next_norm_g.reshape(1, n).astype(F32))
        out_shape = [out_shape, jax.ShapeDtypeStruct((m, n), BF16),
                     jax.ShapeDtypeStruct((m, LANES), F32)]
        out_specs = [out_specs, pl.BlockSpec((bm, bn), lambda i, j, k: (i, j)), stat_spec]
    else:
        out_shape, out_specs = [out_shape], [out_specs]
    grid = (m // bm, n // bn, nk)
    if side_cast is not None:
        src, src_layer = side_cast
        _, src_rows, src_cols = src.shape
        steps = grid[0] * grid[1] * grid[2]
        side_rows = max(src_rows // steps, BF16_SUBLANES)
        n_slices = src_rows // side_rows
        assert src_rows == n_slices * side_rows and steps % n_slices == 0
        hold = steps // n_slices

        def side_block(i, j, k):
            return ((i * grid[1] + j) * grid[2] + k) // hold

        in_specs.append(pl.BlockSpec((None, side_rows, src_cols),
                                     lambda i, j, k: (src_layer, side_block(i, j, k), 0)))
        args.append(src)
        out_shape.append(jax.ShapeDtypeStruct((src_rows, src_cols), BF16))
        out_specs.append(pl.BlockSpec((side_rows, src_cols), lambda i, j, k: (side_block(i, j, k), 0)))
    outs = pl.pallas_call(
        functools.partial(_matmul_kernel, n_lhs=len(a_list), nk=nk, relu2=relu2, has_res=has_res,
                          has_row_ss=row_ss is not None, emit_norm=emit_norm,
                          has_side=side_cast is not None, norm_dim=kdim),
        out_shape=out_shape,
        grid=grid,
        in_specs=in_specs,
        out_specs=out_specs,
        scratch_shapes=[pltpu.VMEM((bm, bn), F32)] if nk > 1 else [],
        compiler_params=_cparams("parallel", "arbitrary" if emit_norm else "parallel", "arbitrary"),
        name=name,
    )(*args)
    return outs[0] if len(outs) == 1 else tuple(outs)


def _sb_kernel(q_ref, k_ref, v_ref, g_ref, o_ref, *, blk, heads):
    i = pl.program_id(2)
    scale = SB_HEAD_DIM ** -0.5
    hd = SB_HEAD_DIM
    qs = [q_ref[0, :, h * hd:(h + 1) * hd].astype(BF16) for h in range(heads)]
    row = lax.broadcasted_iota(jnp.int32, (blk, blk), 0)
    col = lax.broadcasted_iota(jnp.int32, (blk, blk), 1)
    before = col < row
    r2 = lax.broadcasted_iota(jnp.int32, (blk, 2 * blk), 0)
    c2 = lax.broadcasted_iota(jnp.int32, (blk, 2 * blk), 1)
    later = jnp.where((r2 > c2) | (c2 >= blk), 1.0, 0.0).astype(BF16)

    def blocks(specs, carry):
        hs = range(heads)
        units = [(s, h) for s in range(len(specs)) for h in hs]
        starts = [pl.multiple_of(j * blk, blk) for j, _ in specs]
        kbs = [k_ref[0, pl.ds(starts[s], blk), h * hd:(h + 1) * hd].astype(BF16) for s, h in units]
        vbs = [v_ref[0, pl.ds(starts[s], blk), h * hd:(h + 1) * hd].astype(BF16) for s, h in units]
        zs = [_dot_nt(qs[h], kbs[u]) * scale for u, (s, h) in enumerate(units)]
        log_betas = [jnp.minimum(z, 0.0) - jnp.log(1.0 + jnp.exp(-jnp.abs(z))) for z in zs]
        log_keeps = [lb - z for lb, z in zip(log_betas, zs)]
        log_keeps = [lk if specs[s][1] is None else jnp.where(specs[s][1], lk, 0.0)
                     for lk, (s, h) in zip(log_keeps, units)]
        cs_all = _dot_select_cols(jnp.concatenate(log_keeps, axis=0), later)
        css = [cs_all[u * blk:(u + 1) * blk] for u in range(len(units))]
        accs = [carry[2 * h] for h in hs]
        cums = [carry[2 * h + 1] for h in hs]
        for u, (s, h) in enumerate(units):
            w = jnp.exp(log_betas[u] + cums[h] + css[u][:, :blk])
            if specs[s][1] is not None:
                w = jnp.where(specs[s][1], w, 0.0)
            accs[h] = accs[h] + _dot(w.astype(BF16), vbs[u])
            cums[h] = cums[h] + css[u][:, blk:]
        out = []
        for h in hs:
            out += [accs[h], cums[h]]
        return tuple(out)

    carry = (jnp.zeros((blk, hd), F32), jnp.zeros((blk, blk), F32)) * heads
    has_previous = (jnp.zeros((blk, blk), jnp.int32) + i) > 0
    carry = blocks([(i, before), (jnp.maximum(i - 1, 0), has_previous)], carry)

    def alive(state):
        j = state[0]
        top = state[2]
        for h in range(1, heads):
            top = jnp.maximum(top, state[2 + 2 * h])
        return (j >= 0) & (jnp.max(top) > SB_LOG_WEIGHT_CUTOFF)

    def body(state):
        j = state[0]
        return (j - 1,) + blocks([(j, None)], state[1:])

    state = lax.while_loop(alive, body, (i - 2,) + carry)
    for h in range(heads):
        acc = state[1 + 2 * h]
        ms = jnp.mean(acc * acc, axis=-1, keepdims=True)
        o_ref[0, :, h * hd:(h + 1) * hd] = (
            acc * lax.rsqrt(ms + NORM_EPS) * g_ref[:, h * hd:(h + 1) * hd]).astype(o_ref.dtype)


def sb_attention(proj_sb, norm_g, *, blk=128, heads=8):
    b, t, w3 = proj_sb.shape
    w = w3 // 3
    bw = heads * SB_HEAD_DIM
    nh = w // bw
    blk = min(blk, t)
    return pl.pallas_call(
        functools.partial(_sb_kernel, blk=blk, heads=heads),
        out_shape=jax.ShapeDtypeStruct((b, t, w), BF16),
        grid=(b, nh, t // blk),
        in_specs=[pl.BlockSpec((1, blk, bw), lambda bi, h, i: (bi, i, h)),
                  pl.BlockSpec((1, t, bw), lambda bi, h, i: (bi, 0, nh + h)),
                  pl.BlockSpec((1, t, bw), lambda bi, h, i: (bi, 0, 2 * nh + h)),
                  pl.BlockSpec((1, bw), lambda bi, h, i: (0, h))],
        out_specs=pl.BlockSpec((1, blk, bw), lambda bi, h, i: (bi, i, h)),
        compiler_params=_cparams("parallel", "parallel", "arbitrary"),
        name="sb_attention",
    )(proj_sb, proj_sb, proj_sb, norm_g.reshape(1, w).astype(F32))


def _hg_level_halves(chunk):
    halves = []
    m = chunk // 2
    while m >= 1:
        halves.append(m)
        m //= 2
    return halves


HG_ROW_TILE = 8


def _hg_decay_selectors(chunk):
    t = lax.broadcasted_iota(jnp.int32, (chunk, chunk), 0)
    j = lax.broadcasted_iota(jnp.int32, (chunk, chunk), 1)
    mats = [j <= t]
    for m in _hg_level_halves(chunk):
        if m >= HG_ROW_TILE:
            continue
        mid = (t // (2 * m)) * (2 * m) + m - 1
        upper = (t % (2 * m)) >= m
        mats.append((upper & (j > mid) & (j <= t)) | (~upper & (j > t) & (j <= mid)))
    return jnp.concatenate([jnp.where(mm, 1.0, 0.0).astype(BF16) for mm in mats], axis=0)


def _hg_level_decays(sums, chunk):
    b = sums[0:chunk]
    out, small = [], 0
    for m in _hg_level_halves(chunk):
        if m >= HG_ROW_TILE:
            mids = [jnp.broadcast_to(b[s + m - 1:s + m, :], (2 * m, b.shape[1]))
                    for s in range(0, chunk, 2 * m)]
            b_mid = mids[0] if len(mids) == 1 else jnp.concatenate(mids, axis=0)
            out.append(jnp.exp(-jnp.abs(b - b_mid)))
        else:
            small += 1
            out.append(jnp.exp(sums[small * chunk:(small + 1) * chunk]))
    return out


def _hg_kernel(q_ref, f_ref, i_ref, g_ref, lbp_ref, ng_ref, o_ref, st_ref, *, chunk, layer, heads):
    hd = HG_HEAD_DIM
    tb = q_ref.shape[1]

    @pl.when(pl.program_id(2) == 0)
    def _():
        st_ref[...] = jnp.zeros_like(st_ref)

    halves = _hg_level_halves(chunk)
    sel = _hg_decay_selectors(chunk)

    p = lbp_ref[...]
    e = jnp.exp(p - jnp.max(p, axis=0, keepdims=True))
    probs = e / jnp.sum(e, axis=0, keepdims=True)
    lb = jnp.zeros((1, heads * hd), F32)
    for l in range(1, layer + 1):
        lb = lb + probs[l:l + 1, :]

    trow = lax.broadcasted_iota(jnp.int32, (chunk, heads * hd), 0)
    ts = lax.broadcasted_iota(jnp.int32, (chunk, chunk), 0)
    ss = lax.broadcasted_iota(jnp.int32, (chunk, chunk), 1)
    ng = ng_ref[...]

    def step(c, _):
        start = pl.multiple_of(c * chunk, chunk)
        rows = pl.ds(start, chunk)
        q2 = q_ref[0, rows, :]
        fr = f_ref[0, rows, :]
        v2 = i_ref[0, rows, :]
        gate = g_ref[0, rows, :]
        f = lb + (1.0 - lb) * jax.nn.sigmoid(fr)
        log_f = jnp.log(jnp.maximum(f, HG_MIN_F))
        key2 = (1.0 - lb) * jax.nn.sigmoid(-fr)
        sums = _dot_select_rows(sel, log_f)
        b = sums[0:chunk]
        b_last = b[chunk - 1:chunk, :]
        level_decay = _hg_level_decays(sums, chunk)
        silu = gate * jax.nn.sigmoid(gate)

        hs = range(heads)
        lanes = [slice(h * hd, (h + 1) * hd) for h in hs]
        states = [st_ref[h] for h in hs]
        vbs = [v2[:, l].astype(BF16) for l in lanes]
        scores = [None] * heads
        for li, m in enumerate(halves):
            el = level_decay[li]
            upper = (trow % (2 * m)) >= m
            ql = jnp.where(upper, q2 * el, 0.0).astype(BF16)
            kl = jnp.where(upper, 0.0, key2 * el).astype(BF16)
            for h in hs:
                sl = _dot_nt(ql[:, lanes[h]], kl[:, lanes[h]])
                if 2 * m < chunk:
                    sl = jnp.where((ts // (2 * m)) == (ss // (2 * m)), sl, 0.0)
                scores[h] = sl if scores[h] is None else scores[h] + sl
        q_pre = (q2 * jnp.exp(b)).astype(BF16)
        k_suf = (key2 * jnp.exp(b_last - b)).astype(BF16)
        qk = q2 * key2
        inter = [_dot_nt(q_pre[:, lanes[h]], states[h].astype(BF16)) for h in hs]
        intra = [_dot(scores[h].astype(BF16), vbs[h]) for h in hs]
        upd = [_dot_tn(vbs[h], k_suf[:, lanes[h]]) for h in hs]
        decay_all = jnp.exp(b_last)
        for h in hs:
            l = lanes[h]
            st_ref[h] = states[h] * decay_all[:, l] + upd[h]
            diag = jnp.sum(qk[:, l], axis=-1, keepdims=True)
            o = intra[h] + diag * v2[:, l] + inter[h]
            ms = jnp.mean(o * o, axis=-1, keepdims=True)
            o = o * lax.rsqrt(ms + NORM_EPS) * ng
            o_ref[0, rows, l] = (o * silu[:, l]).astype(o_ref.dtype)
        return 0

    lax.fori_loop(0, tb // chunk, step, 0, unroll=2)


def hgrn2(proj_hg, lb_param, norm_g, *, layer, chunk=128, heads=4, tb=512):
    b, t, w4 = proj_hg.shape
    w = w4 // 4
    bw = heads * HG_HEAD_DIM
    nh = w // bw
    tb = min(tb, t)
    chunk = min(chunk, tb)
    depth = lb_param.shape[0]

    def col(g):
        return pl.BlockSpec((1, tb, bw), lambda bi, h, ti: (bi, ti, g * nh + h))

    return pl.pallas_call(
        functools.partial(_hg_kernel, chunk=chunk, layer=layer, heads=heads),
        out_shape=jax.ShapeDtypeStruct((b, t, w), BF16),
        grid=(b, nh, t // tb),
        in_specs=[col(0), col(1), col(2), col(3),
                  pl.BlockSpec((depth, bw), lambda bi, h, ti: (0, h)),
                  pl.BlockSpec((1, HG_HEAD_DIM), lambda bi, h, ti: (0, 0))],
        out_specs=pl.BlockSpec((1, tb, bw), lambda bi, h, ti: (bi, ti, h)),
        scratch_shapes=[pltpu.VMEM((heads, HG_HEAD_DIM, HG_HEAD_DIM), F32)],
        compiler_params=_cparams("parallel", "parallel", "arbitrary"),
        name="hgrn2",
    )(proj_hg, proj_hg, proj_hg, proj_hg, lb_param.astype(F32),
      norm_g.reshape(1, HG_HEAD_DIM).astype(F32))


RW_GROUP_HEADS = 4
RW_GROUP = RW_GROUP_HEADS * RW_HEAD_DIM


def _head_ones(n, head):
    r = lax.broadcasted_iota(jnp.int32, (n, n), 0)
    c = lax.broadcasted_iota(jnp.int32, (n, n), 1)
    return jnp.where((r // head) == (c // head), 1.0, 0.0).astype(BF16)


def _head_sums(x, ones_bd):
    n = ones_bd.shape[0]
    rows, width = x.shape
    if width == n:
        return _dot_select_cols(x, ones_bd)
    stacked = jnp.concatenate([x[:, s:s + n] for s in range(0, width, n)], axis=0)
    sums = _dot_select_cols(stacked, ones_bd)
    return jnp.concatenate([sums[g * rows:(g + 1) * rows] for g in range(width // n)], axis=1)


def _softplus(y):
    return jnp.maximum(y, 0.0) + jnp.log(1.0 + jnp.exp(-jnp.abs(y)))


def _rw_prepare_block(i, x_ref, xp_ref, lo_ref, lop_ref, mu_ref, mulo_ref, w0_ref, a0_ref, kk_ref,
                      ka_ref, rk_ref, w2w_ref, w2a_ref, w2g_ref,
                      r_out, lw_out, k_out, v_out, kk_out, kb_out, g_out, bonus_out):
    width = r_out.shape[1]

    def token_shift(cur_ref, prev_ref, mix_ref):
        x = cur_ref[0]
        prev_row = jnp.where(i == 0, 0.0, prev_ref[0, 7:8, :])
        row = lax.broadcasted_iota(jnp.int32, x.shape, 0)
        prev = jnp.where(row == 0, prev_row, pltpu.roll(x, 1, axis=0))
        return x + mix_ref[...] * (prev - x)

    xs = token_shift(x_ref, xp_ref, mu_ref)
    lora = token_shift(lo_ref, lop_ref, mulo_ref)
    w_log = -_softplus(-(w0_ref[...] + _dot(jnp.tanh(lora).astype(BF16), w2w_ref[...]))) - 0.5
    log_decay = -jnp.exp(w_log)
    a = jax.nn.sigmoid(a0_ref[...] + _dot(lora.astype(BF16), w2a_ref[...]))
    g = _dot(jax.nn.sigmoid(lora).astype(BF16), w2g_ref[...])

    r = xs[:, :width]
    k = xs[:, width:2 * width]
    v = xs[:, 2 * width:3 * width]
    ones_bd = _head_ones(RW_GROUP, RW_HEAD_DIM)
    kk = k * kk_ref[...]
    kk = kk * lax.rsqrt(jnp.maximum(_head_sums(kk * kk, ones_bd), 1e-24))
    k = k * (1.0 + (a - 1.0) * ka_ref[...])
    bonus = _head_sums(r * k * rk_ref[...], ones_bd) * v

    r_out[...] = r
    lw_out[...] = log_decay
    k_out[...] = k
    v_out[...] = v
    kk_out[...] = kk
    kb_out[...] = kk * a
    g_out[...] = g
    bonus_out[...] = bonus


def _tile_rows(x, n):
    return jnp.concatenate([x] * n, axis=0)


def _rw_kernel(x_ref, xp_ref, lo_ref, lop_ref, mu_ref, mulo_ref, w0_ref, a0_ref, kkp_ref, ka_ref,
               rk_ref, w2w_ref, w2a_ref, w2g_ref, lnw_ref, lnb_ref, o_ref,
               ht_ref, r_ref, lw_ref, k_ref, v_ref, kk_ref, kb_ref, g_ref, bonus_ref,
               *, chunk):
    nh, hd, gw = RW_GROUP_HEADS, RW_HEAD_DIM, RW_GROUP
    wide = nh * chunk
    tb, width = r_ref.shape
    groups = width // gw
    ti = pl.program_id(1)

    @pl.when(ti == 0)
    def _():
        ht_ref[...] = jnp.zeros_like(ht_ref)

    _rw_prepare_block(ti, x_ref, xp_ref, lo_ref, lop_ref, mu_ref, mulo_ref, w0_ref, a0_ref, kkp_ref,
                      ka_ref, rk_ref, w2w_ref, w2a_ref, w2g_ref,
                      r_ref, lw_ref, k_ref, v_ref, kk_ref, kb_ref, g_ref, bonus_ref)

    def iota(shape, d):
        return lax.broadcasted_iota(jnp.int32, shape, d)

    ltri = jnp.where(iota((chunk, chunk), 1) <= iota((chunk, chunk), 0), 1.0, 0.0).astype(BF16)
    row_cw = iota((chunk, wide), 0)
    pos_cw = iota((chunk, wide), 1) % chunk
    strict = pos_cw < row_cw
    incl = pos_cw <= row_cw
    eye_t = jnp.where(pos_cw == row_cw, 1.0, 0.0)
    off_diag = []
    s = 1
    while s < chunk:
        off_diag.append(((row_cw // (2 * s)) == (pos_cw // (2 * s)))
                        & ((row_cw % (2 * s)) >= s) & ((pos_cw % (2 * s)) < s))
        s *= 2
    bd_wk = (iota((wide, gw), 0) // chunk) == (iota((wide, gw), 1) // hd)
    bd_ww = (iota((wide, wide), 0) // chunk) == (iota((wide, wide), 1) // chunk)
    head_mask = (iota((gw, gw), 0) // hd) == (iota((gw, gw), 1) // hd)
    ones_bd = _head_ones(gw, hd)

    def expand_k(x):
        return jnp.where(bd_wk, _tile_rows(x, nh), 0.0).astype(BF16)

    def expand_w(x):
        return jnp.where(bd_ww, _tile_rows(x, nh), 0.0)

    def mm(a, b):
        return _dot(a.astype(BF16), b.astype(BF16))

    def step(c, _):
        rows = pl.ds(pl.multiple_of(c * chunk, chunk), chunk)
        gs = range(groups)
        lanes = [slice(gi * gw, (gi + 1) * gw) for gi in gs]
        lw = lw_ref[rows, :]
        gam = _dot_select_rows(ltri, lw)
        gam_end = gam[chunk - 1:chunk, :]
        e_neg = jnp.exp(-gam)
        e_suf = jnp.exp(gam_end - gam)
        k_all = k_ref[rows, :]
        kb_all = kb_ref[rows, :]
        v_all = v_ref[rows, :]
        a_t = -kk_ref[rows, :] * jnp.exp(gam - lw)
        r_t = r_ref[rows, :] * jnp.exp(gam)
        b_t = kb_all * e_neg
        k_t = k_all * e_neg
        b_h = kb_all * e_suf
        k_h = k_all * e_suf
        h_decay = jnp.exp(gam_end)

        lhs = [jnp.concatenate([a_t[:, l], r_t[:, l]], axis=0).astype(BF16) for l in lanes]
        sb = [_dot_nt(lhs[g], expand_k(b_t[:, lanes[g]])) for g in gs]
        sk = [_dot_nt(lhs[g], expand_k(k_t[:, lanes[g]])) for g in gs]
        a_ab = [jnp.where(strict, s[:chunk], 0.0) for s in sb]
        a_rb = [jnp.where(incl, s[chunk:], 0.0) for s in sb]
        a_ak = [jnp.where(strict, s[:chunk], 0.0) for s in sk]
        a_rk = [jnp.where(incl, s[chunk:], 0.0) for s in sk]

        hts = [ht_ref[g] for g in gs]
        from_state = [_dot_nt(lhs[g], hts[g].astype(BF16)) for g in gs]
        from_v = [_dot(jnp.concatenate([a_ak[g], a_rk[g]], axis=0).astype(BF16),
                       expand_k(v_all[:, lanes[g]])) for g in gs]

        p_acc = [eye_t + jnp.where(off_diag[0], a, 0.0) for a in a_ab]
        for li in range(1, len(off_diag)):
            left = [mm(p_acc[g], expand_w(jnp.where(off_diag[li], a_ab[g], 0.0))) for g in gs]
            p_acc = [p_acc[g] + mm(left[g], expand_w(p_acc[g])) for g in gs]

        u = [_dot(p_acc[g].astype(BF16), expand_k(from_state[g][:chunk] + from_v[g][:chunk]))
             for g in gs]
        y = [from_state[g][chunk:] + from_v[g][chunk:] + _dot(a_rb[g].astype(BF16), expand_k(u[g]))
             for g in gs]
        upd = [_dot_tn(jnp.concatenate([u[g], v_all[:, lanes[g]]], axis=0).astype(BF16),
                       jnp.concatenate([b_h[:, lanes[g]], k_h[:, lanes[g]]], axis=0).astype(BF16))
               for g in gs]
        for g in gs:
            ht_ref[g] = hts[g] * h_decay[:, lanes[g]] + jnp.where(head_mask, upd[g], 0.0)

        y_all = jnp.concatenate(y, axis=0)
        yc_all = y_all - _head_sums(y_all, ones_bd) * (1.0 / hd)
        var_all = _head_sums(yc_all * yc_all, ones_bd) * (1.0 / hd)
        yn_all = yc_all * lax.rsqrt(var_all + RW_GN_EPS)
        for g in gs:
            l = lanes[g]
            yn = yn_all[g * chunk:(g + 1) * chunk] * lnw_ref[:, l] + lnb_ref[:, l]
            out = (yn + bonus_ref[rows, l]) * g_ref[rows, l]
            o_ref[0, rows, l] = out.astype(o_ref.dtype)
        return 0

    lax.fori_loop(0, tb // chunk, step, 0, unroll=2)


def rwkv7(rkv, lora, mu, w0, w_w2, a0, w_a2, w_g2, k_k, k_a, r_k, lnx_w, lnx_b, *, chunk=64, tb=256):
    b, t, cols = rkv.shape
    width = cols // 3
    tb = min(tb, t)

    def padded(w2, first_row):
        full = jnp.zeros((RW_LORA_COLS, width), F32)
        return lax.dynamic_update_slice(full, w2.astype(F32), (first_row, 0)).astype(BF16)

    w2w = padded(w_w2, 0)
    w2a = padded(w_a2, RW_DECAY_RANK)
    w2g = padded(w_g2, RW_DECAY_RANK + RW_AAA_RANK)

    def vec(p):
        return p.reshape(1, -1).astype(F32)

    def vspec(n):
        return pl.BlockSpec((1, n), lambda bi, i: (0, 0))

    def cur(n):
        return pl.BlockSpec((1, tb, n), lambda bi, i: (bi, i, 0))

    def prev(n):
        return pl.BlockSpec((1, 8, n), lambda bi, i: (bi, jnp.maximum(i * (tb // 8) - 1, 0), 0))

    wspec = pl.BlockSpec((RW_LORA_COLS, width), lambda bi, i: (0, 0))
    return pl.pallas_call(
        functools.partial(_rw_kernel, chunk=chunk),
        out_shape=jax.ShapeDtypeStruct((b, t, width), BF16),
        grid=(b, t // tb),
        in_specs=[cur(cols), prev(cols), cur(RW_LORA_COLS), prev(RW_LORA_COLS),
                  vspec(cols), vspec(RW_LORA_COLS),
                  vspec(width), vspec(width), vspec(width), vspec(width), vspec(width),
                  wspec, wspec, wspec, vspec(width), vspec(width)],
        out_specs=cur(width),
        scratch_shapes=[pltpu.VMEM((width // RW_GROUP, RW_GROUP, RW_GROUP), F32)]
                       + [pltpu.VMEM((tb, width), F32)] * 8,
        compiler_params=_cparams("parallel", "arbitrary"),
        name="rwkv7",
    )(rkv, rkv, lora, lora, vec(mu[:cols]), vec(mu[cols:]), vec(w0), vec(a0), vec(k_k), vec(k_a),
      vec(r_k), w2w, w2a, w2g, vec(lnx_w), vec(lnx_b))


def kernel(x, norm1_g, w_in, sb_norm_g, hg_lb_param, hg_norm_g, rw_mu, rw_w0, rw_w_w2, rw_a0,
           rw_w_a2, rw_w_g2, rw_k_k, rw_k_a, rw_r_k, rw_lnx_w, rw_lnx_b, w_out, norm2_g,
           w_ff_in, w_ff_out, final_g):
    b, t, d = x.shape
    depth = norm1_g.shape[0]
    sb_w = sb_norm_g.shape[1]
    hg_w = hg_lb_param.shape[1]
    rw_w = rw_lnx_w.shape[1]
    sb_cols, hg_cols, rkv_cols = 3 * sb_w, 4 * hg_w, 3 * rw_w
    m = b * t
    xf = x.reshape(m, d)
    w_in_l = w_in[:1].astype(BF16)
    h, ss = rmsnorm(xf, norm1_g[0], BF16), None
    for l in range(depth):
        proj = functools.partial(matmul, h, w_in_l, bm=1024, row_ss=ss)
        p_sb = proj(bn=1024, col0=0, n=sb_cols, out_dtype=BF16, name="proj_sb").reshape(b, t, -1)
        p_hg, w_out_l = proj(bn=1024, col0=sb_cols, n=hg_cols, side_cast=(w_out, l), name="proj_hg")
        p_hg = p_hg.reshape(b, t, -1)
        p_rkv = proj(bn=1024, col0=sb_cols + hg_cols, n=rkv_cols, name="proj_rkv").reshape(b, t, -1)
        p_lora = proj(bn=RW_LORA_COLS, col0=sb_cols + hg_cols + rkv_cols, n=RW_LORA_COLS,
                      name="proj_lora").reshape(b, t, -1)
        o_sb = sb_attention(p_sb, sb_norm_g[l])
        o_hg = hgrn2(p_hg, hg_lb_param, hg_norm_g[l], layer=l)
        o_rw = rwkv7(p_rkv, p_lora, rw_mu[l], rw_w0[l], rw_w_w2[l], rw_a0[l], rw_w_a2[l],
                     rw_w_g2[l], rw_k_k[l], rw_k_a[l], rw_r_k[l], rw_lnx_w[l], rw_lnx_b[l])
        mix = [o.reshape(m, -1) for o in (o_sb, o_hg, o_rw)]
        xf, h, ss, w_ff_in_l = matmul(mix, w_out_l, bm=1024, bn=512, residual=xf,
                                      next_norm_g=norm2_g[l], side_cast=(w_ff_in, l), name="out_proj")
        ff, w_ff_out_l = matmul(h, w_ff_in_l, bm=1024, bn=1024, relu2=True, row_ss=ss, out_dtype=BF16,
                                side_cast=(w_ff_out, l), name="ff_in")
        ff_out = functools.partial(matmul, ff, w_ff_out_l, bm=1024, bn=1024, bk=2048, residual=xf)
        if l + 1 < depth:
            xf, h, ss, w_in_l = ff_out(next_norm_g=norm1_g[l + 1], side_cast=(w_in, l + 1),
                                       name="ff_out_norm")
        else:
            xf = ff_out(name="ff_out")
    return rmsnorm(xf, final_g, F32).reshape(b, t, d)
```

```python
import functools

import jax
import jax.numpy as jnp
from jax import lax
from jax.experimental import pallas as pl
from jax.experimental.pallas import tpu as pltpu

F32 = jnp.float32
BF16 = jnp.bfloat16

NORM_EPS = 1e-5
V7X_VMEM_BYTES = 64 * 1024 * 1024
VMEM_LIMIT = V7X_VMEM_BYTES - 8 * 1024 * 1024
LANES = 128
BF16_SUBLANES = 16

SB_HEAD_DIM = 128
SB_LOG_WEIGHT_CUTOFF = -90.0
HG_HEAD_DIM = 128
HG_MIN_F = 1e-30
RW_HEAD_DIM = 64
RW_GN_EPS = 64e-5
RW_DECAY_RANK = 96
RW_AAA_RANK = 96
RW_GATE_RANK = 64
RW_LORA_COLS = RW_DECAY_RANK + RW_AAA_RANK + RW_GATE_RANK


def _cparams(*sem):
    return pltpu.CompilerParams(dimension_semantics=sem, vmem_limit_bytes=VMEM_LIMIT)


def _dot(a, b):
    return jnp.dot(a, b, preferred_element_type=F32)


def _dot_nt(a, b):
    return lax.dot_general(a, b, (((1,), (1,)), ((), ())), preferred_element_type=F32)


def _dot_tn(a, b):
    return lax.dot_general(a, b, (((0,), (0,)), ((), ())), preferred_element_type=F32)


def _split2(x):
    hi = x.astype(BF16)
    lo = (x - hi.astype(F32)).astype(BF16)
    return hi, lo


def _dot_select_rows(a01, x):
    hi, lo = _split2(x)
    return _dot(a01, hi) + _dot(a01, lo)


def _dot_select_cols(x, b01):
    hi, lo = _split2(x)
    return _dot(hi, b01) + _dot(lo, b01)


def _rmsnorm_kernel(x_ref, g_ref, o_ref):
    x = x_ref[...]
    ms = jnp.mean(x * x, axis=-1, keepdims=True)
    o_ref[...] = (x * lax.rsqrt(ms + NORM_EPS) * g_ref[...]).astype(o_ref.dtype)


def rmsnorm(x, g, out_dtype, bm=512):
    m, d = x.shape
    bm = min(bm, m)
    return pl.pallas_call(
        _rmsnorm_kernel,
        out_shape=jax.ShapeDtypeStruct((m, d), out_dtype),
        grid=(m // bm,),
        in_specs=[pl.BlockSpec((bm, d), lambda i: (i, 0)),
                  pl.BlockSpec((1, d), lambda i: (0, 0))],
        out_specs=pl.BlockSpec((bm, d), lambda i: (i, 0)),
        compiler_params=_cparams("parallel"),
        name="rmsnorm",
    )(x, g.reshape(1, d).astype(F32))


def _matmul_kernel(*refs, n_lhs, nk, relu2, has_res, has_row_ss, emit_norm, has_side, norm_dim):
    refs = list(refs)
    a_refs = [refs.pop(0) for _ in range(n_lhs)]
    b_ref = refs.pop(0)
    r_ref = refs.pop(0) if has_res else None
    ss_in_ref = refs.pop(0) if has_row_ss else None
    gnext_ref = refs.pop(0) if emit_norm else None
    side_in_ref = refs.pop(0) if has_side else None
    o_ref = refs.pop(0)
    xg_ref, ss_out_ref = (refs.pop(0), refs.pop(0)) if emit_norm else (None, None)
    side_out_ref = refs.pop(0) if has_side else None
    rest = refs
    j = pl.program_id(1)

    if has_side:
        side_out_ref[...] = side_in_ref[...].astype(side_out_ref.dtype)

    def product():
        acc, k0 = None, 0
        for a_ref in a_refs:
            kw = a_ref.shape[1]
            part = _dot(a_ref[...], b_ref[k0:k0 + kw, :])
            acc = part if acc is None else acc + part
            k0 += kw
        return acc

    def finish(acc):
        if has_row_ss:
            acc = acc * lax.rsqrt(ss_in_ref[:, :1] * (1.0 / norm_dim) + NORM_EPS)
        if relu2:
            acc = jnp.square(jnp.maximum(acc, 0.0))
        if has_res:
            acc = acc + r_ref[...]
        o_ref[...] = acc.astype(o_ref.dtype)
        if emit_norm:
            xg_ref[...] = (acc * gnext_ref[...]).astype(xg_ref.dtype)
            part = jnp.broadcast_to(jnp.sum(acc * acc, axis=-1, keepdims=True), ss_out_ref.shape)

            @pl.when(j == 0)
            def _():
                ss_out_ref[...] = part

            @pl.when(j > 0)
            def _():
                ss_out_ref[...] += part

    if nk == 1:
        finish(product())
        return

    acc_ref = rest[0]
    k = pl.program_id(2)

    @pl.when(k == 0)
    def _():
        acc_ref[...] = product()

    @pl.when((k > 0) & (k < nk - 1))
    def _():
        acc_ref[...] += product()

    @pl.when(k == nk - 1)
    def _():
        finish(acc_ref[...] + product())


def matmul(a, b, *, bm, bn, bk=None, layer=0, col0=0, n=None, relu2=False, residual=None,
           row_ss=None, next_norm_g=None, side_cast=None, out_dtype=F32, name="matmul"):
    a_list = list(a) if isinstance(a, (list, tuple)) else [a]
    m = a_list[0].shape[0]
    kdim = sum(x.shape[1] for x in a_list)
    b_rows, b_cols = b.shape[-2:]
    n = b_cols - col0 if n is None else n
    bm, bn = min(bm, m), min(bn, n)
    bk = kdim if bk is None else min(bk, kdim)
    assert b_rows == kdim and m % bm == 0 and n % bn == 0 and kdim % bk == 0 and col0 % bn == 0
    nk = kdim // bk
    assert nk == 1 or len(a_list) == 1
    has_res = residual is not None
    jb0 = col0 // bn
    if len(a_list) == 1:
        in_specs = [pl.BlockSpec((bm, bk), lambda i, j, k: (i, k))]
    else:
        in_specs = [pl.BlockSpec((bm, x.shape[1]), lambda i, j, k: (i, 0)) for x in a_list]
    if b.ndim == 3:
        in_specs.append(pl.BlockSpec((None, bk, bn), lambda i, j, k: (layer, k, jb0 + j)))
    else:
        in_specs.append(pl.BlockSpec((bk, bn), lambda i, j, k: (k, jb0 + j)))
    args = a_list + [b]
    if has_res:
        in_specs.append(pl.BlockSpec((bm, bn), lambda i, j, k: (i, j)))
        args.append(residual)
    stat_spec = pl.BlockSpec((bm, LANES), lambda i, j, k: (i, 0))
    if row_ss is not None:
        in_specs.append(stat_spec)
        args.append(row_ss)
    emit_norm = next_norm_g is not None
    out_shape = jax.ShapeDtypeStruct((m, n), out_dtype)
    out_specs = pl.BlockSpec((bm, bn), lambda i, j, k: (i, j))
    if emit_norm:
        assert col0 == 0 and n == b_cols
        in_specs.append(pl.BlockSpec((1, bn), lambda i, j, k: (0, j)))
        args.append(next_norm_g.reshape(1, n).astype(F32))
        out_shape = [out_shape, jax.ShapeDtypeStruct((m, n), BF16),
                     jax.ShapeDtypeStruct((m, LANES), F32)]
        out_specs = [out_specs, pl.BlockSpec((bm, bn), lambda i, j, k: (i, j)), stat_spec]
    else:
        out_shape, out_specs = [out_shape], [out_specs]
    grid = (m // bm, n // bn, nk)
    if side_cast is not None:
        src, src_layer = side_cast
        _, src_rows, src_cols = src.shape
        steps = grid[0] * grid[1] * grid[2]
        side_rows = max(src_rows // steps, BF16_SUBLANES)
        n_slices = src_rows // side_rows
        assert src_rows == n_slices * side_rows and steps % n_slices == 0
        hold = steps // n_slices

        def side_block(i, j, k):
            return ((i * grid[1] + j) * grid[2] + k) // hold

        in_specs.append(pl.BlockSpec((None, side_rows, src_cols),
                                     lambda i, j, k: (src_layer, side_block(i, j, k), 0)))
        args.append(src)
        out_shape.append(jax.ShapeDtypeStruct((src_rows, src_cols), BF16))
        out_specs.append(pl.BlockSpec((side_rows, src_cols), lambda i, j, k: (side_block(i, j, k), 0)))
    outs = pl.pallas_call(
        functools.partial(_matmul_kernel, n_lhs=len(a_list), nk=nk, relu2=relu2, has_res=has_res,
                          has_row_ss=row_ss is not None, emit_norm=emit_norm,
                          has_side=side_cast is not None, norm_dim=kdim),
        out_shape=out_shape,
        grid=grid,
        in_specs=in_specs,
        out_specs=out_specs,
        scratch_shapes=[pltpu.VMEM((bm, bn), F32)] if nk > 1 else [],
        compiler_params=_cparams("parallel", "arbitrary" if emit_norm else "parallel", "arbitrary"),
        name=name,
    )(*args)
    return outs[0] if len(outs) == 1 else tuple(outs)


def _sb_kernel(q_ref, k_ref, v_ref, g_ref, o_ref, *, blk, heads):
    i = pl.program_id(2)
    scale = SB_HEAD_DIM ** -0.5
    hd = SB_HEAD_DIM
    qs = [q_ref[0, :, h * hd:(h + 1) * hd].astype(BF16) for h in range(heads)]
    row = lax.broadcasted_iota(jnp.int32, (blk, blk), 0)
    col = lax.broadcasted_iota(jnp.int32, (blk, blk), 1)
    before = col < row
    r2 = lax.broadcasted_iota(jnp.int32, (blk, 2 * blk), 0)
    c2 = lax.broadcasted_iota(jnp.int32, (blk, 2 * blk), 1)
    later = jnp.where((r2 > c2) | (c2 >= blk), 1.0, 0.0).astype(BF16)

    def blocks(specs, carry):
        hs = range(heads)
        units = [(s, h) for s in range(len(specs)) for h in hs]
        starts = [pl.multiple_of(j * blk, blk) for j, _ in specs]
        kbs = [k_ref[0, pl.ds(starts[s], blk), h * hd:(h + 1) * hd].astype(BF16) for s, h in units]
        vbs = [v_ref[0, pl.ds(starts[s], blk), h * hd:(h + 1) * hd].astype(BF16) for s, h in units]
        zs = [_dot_nt(qs[h], kbs[u]) * scale for u, (s, h) in enumerate(units)]
        log_betas = [jnp.minimum(z, 0.0) - jnp.log(1.0 + jnp.exp(-jnp.abs(z))) for z in zs]
        log_keeps = [lb - z for lb, z in zip(log_betas, zs)]
        log_keeps = [lk if specs[s][1] is None else jnp.where(specs[s][1], lk, 0.0)
                     for lk, (s, h) in zip(log_keeps, units)]
        cs_all = _dot_select_cols(jnp.concatenate(log_keeps, axis=0), later)
        css = [cs_all[u * blk:(u + 1) * blk] for u in range(len(units))]
        accs = [carry[2 * h] for h in hs]
        cums = [carry[2 * h + 1] for h in hs]
        for u, (s, h) in enumerate(units):
            w = jnp.exp(log_betas[u] + cums[h] + css[u][:, :blk])
            if specs[s][1] is not None:
                w = jnp.where(specs[s][1], w, 0.0)
            accs[h] = accs[h] + _dot(w.astype(BF16), vbs[u])
            cums[h] = cums[h] + css[u][:, blk:]
        out = []
        for h in hs:
            out += [accs[h], cums[h]]
        return tuple(out)

    carry = (jnp.zeros((blk, hd), F32), jnp.zeros((blk, blk), F32)) * heads
    has_previous = (jnp.zeros((blk, blk), jnp.int32) + i) > 0
    carry = blocks([(i, before), (jnp.maximum(i - 1, 0), has_previous)], carry)

    def alive(state):
        j = state[0]
        top = state[2]
        for h in range(1, heads):
            top = jnp.maximum(top, state[2 + 2 * h])
        return (j >= 0) & (jnp.max(top) > SB_LOG_WEIGHT_CUTOFF)

    def body(state):
        j = state[0]
        return (j - 1,) + blocks([(j, None)], state[1:])

    state = lax.while_loop(alive, body, (i - 2,) + carry)
    for h in range(heads):
        acc = state[1 + 2 * h]
        ms = jnp.mean(acc * acc, axis=-1, keepdims=True)
        o_ref[0, :, h * hd:(h + 1) * hd] = (
            acc * lax.rsqrt(ms + NORM_EPS) * g_ref[:, h * hd:(h + 1) * hd]).astype(o_ref.dtype)


def sb_attention(proj_sb, norm_g, *, blk=128, heads=8):
    b, t, w3 = proj_sb.shape
    w = w3 // 3
    bw = heads * SB_HEAD_DIM
    nh = w // bw
    blk = min(blk, t)
    return pl.pallas_call(
        functools.partial(_sb_kernel, blk=blk, heads=heads),
        out_shape=jax.ShapeDtypeStruct((b, t, w), BF16),
        grid=(b, nh, t // blk),
        in_specs=[pl.BlockSpec((1, blk, bw), lambda bi, h, i: (bi, i, h)),
                  pl.BlockSpec((1, t, bw), lambda bi, h, i: (bi, 0, nh + h)),
                  pl.BlockSpec((1, t, bw), lambda bi, h, i: (bi, 0, 2 * nh + h)),
                  pl.BlockSpec((1, bw), lambda bi, h, i: (0, h))],
        out_specs=pl.BlockSpec((1, blk, bw), lambda bi, h, i: (bi, i, h)),
        compiler_params=_cparams("parallel", "parallel", "arbitrary"),
        name="sb_attention",
    )(proj_sb, proj_sb, proj_sb, norm_g.reshape(1, w).astype(F32))


def _hg_level_halves(chunk):
    halves = []
    m = chunk // 2
    while m >= 1:
        halves.append(m)
        m //= 2
    return halves


HG_ROW_TILE = 8


def _hg_decay_selectors(chunk):
    t = lax.broadcasted_iota(jnp.int32, (chunk, chunk), 0)
    j = lax.broadcasted_iota(jnp.int32, (chunk, chunk), 1)
    mats = [j <= t]
    for m in _hg_level_halves(chunk):
        if m >= HG_ROW_TILE:
            continue
        mid = (t // (2 * m)) * (2 * m) + m - 1
        upper = (t % (2 * m)) >= m
        mats.append((upper & (j > mid) & (j <= t)) | (~upper & (j > t) & (j <= mid)))
    return jnp.concatenate([jnp.where(mm, 1.0, 0.0).astype(BF16) for mm in mats], axis=0)


def _hg_level_decays(sums, chunk):
    b = sums[0:chunk]
    out, small = [], 0
    for m in _hg_level_halves(chunk):
        if m >= HG_ROW_TILE:
            mids = [jnp.broadcast_to(b[s + m - 1:s + m, :], (2 * m, b.shape[1]))
                    for s in range(0, chunk, 2 * m)]
            b_mid = mids[0] if len(mids) == 1 else jnp.concatenate(mids, axis=0)
            out.append(jnp.exp(-jnp.abs(b - b_mid)))
        else:
            small += 1
            out.append(jnp.exp(sums[small * chunk:(small + 1) * chunk]))
    return out


def _hg_kernel(q_ref, f_ref, i_ref, g_ref, lbp_ref, ng_ref, o_ref, st_ref, *, chunk, layer, heads):
    hd = HG_HEAD_DIM
    tb = q_ref.shape[1]

    @pl.when(pl.program_id(2) == 0)
    def _():
        st_ref[...] = jnp.zeros_like(st_ref)

    halves = _hg_level_halves(chunk)
    sel = _hg_decay_selectors(chunk)

    p = lbp_ref[...]
    e = jnp.exp(p - jnp.max(p, axis=0, keepdims=True))
    probs = e / jnp.sum(e, axis=0, keepdims=True)
    lb = jnp.zeros((1, heads * hd), F32)
    for l in range(1, layer + 1):
        lb = lb + probs[l:l + 1, :]

    trow = lax.broadcasted_iota(jnp.int32, (chunk, heads * hd), 0)
    ts = lax.broadcasted_iota(jnp.int32, (chunk, chunk), 0)
    ss = lax.broadcasted_iota(jnp.int32, (chunk, chunk), 1)
    ng = ng_ref[...]

    def step(c, _):
        start = pl.multiple_of(c * chunk, chunk)
        rows = pl.ds(start, chunk)
        q2 = q_ref[0, rows, :]
        fr = f_ref[0, rows, :]
        v2 = i_ref[0, rows, :]
        gate = g_ref[0, rows, :]
        f = lb + (1.0 - lb) * jax.nn.sigmoid(fr)
        log_f = jnp.log(jnp.maximum(f, HG_MIN_F))
        key2 = (1.0 - lb) * jax.nn.sigmoid(-fr)
        sums = _dot_select_rows(sel, log_f)
        b = sums[0:chunk]
        b_last = b[chunk - 1:chunk, :]
        level_decay = _hg_level_decays(sums, chunk)
        silu = gate * jax.nn.sigmoid(gate)

        hs = range(heads)
        lanes = [slice(h * hd, (h + 1) * hd) for h in hs]
        states = [st_ref[h] for h in hs]
        vbs = [v2[:, l].astype(BF16) for l in lanes]
        scores = [None] * heads
        for li, m in enumerate(halves):
            el = level_decay[li]
            upper = (trow % (2 * m)) >= m
            ql = jnp.where(upper, q2 * el, 0.0).astype(BF16)
            kl = jnp.where(upper, 0.0, key2 * el).astype(BF16)
            for h in hs:
                sl = _dot_nt(ql[:, lanes[h]], kl[:, lanes[h]])
                if 2 * m < chunk:
                    sl = jnp.where((ts // (2 * m)) == (ss // (2 * m)), sl, 0.0)
                scores[h] = sl if scores[h] is None else scores[h] + sl
        q_pre = (q2 * jnp.exp(b)).astype(BF16)
        k_suf = (key2 * jnp.exp(b_last - b)).astype(BF16)
        qk = q2 * key2
        inter = [_dot_nt(q_pre[:, lanes[h]], states[h].astype(BF16)) for h in hs]
        intra = [_dot(scores[h].astype(BF16), vbs[h]) for h in hs]
        upd = [_dot_tn(vbs[h], k_suf[:, lanes[h]]) for h in hs]
        decay_all = jnp.exp(b_last)
        for h in hs:
            l = lanes[h]
            st_ref[h] = states[h] * decay_all[:, l] + upd[h]
            diag = jnp.sum(qk[:, l], axis=-1, keepdims=True)
            o = intra[h] + diag * v2[:, l] + inter[h]
            ms = jnp.mean(o * o, axis=-1, keepdims=True)
            o = o * lax.rsqrt(ms + NORM_EPS) * ng
            o_ref[0, rows, l] = (o * silu[:, l]).astype(o_ref.dtype)
        return 0

    lax.fori_loop(0, tb // chunk, step, 0, unroll=4)


def hgrn2(proj_hg, lb_param, norm_g, *, layer, chunk=128, heads=4, tb=512):
    b, t, w4 = proj_hg.shape
    w = w4 // 4
    bw = heads * HG_HEAD_DIM
    nh = w // bw
    tb = min(tb, t)
    chunk = min(chunk, tb)
    depth = lb_param.shape[0]

    def col(g):
        return pl.BlockSpec((1, tb, bw), lambda bi, h, ti: (bi, ti, g * nh + h))

    return pl.pallas_call(
        functools.partial(_hg_kernel, chunk=chunk, layer=layer, heads=heads),
        out_shape=jax.ShapeDtypeStruct((b, t, w), BF16),
        grid=(b, nh, t // tb),
        in_specs=[col(0), col(1), col(2), col(3),
                  pl.BlockSpec((depth, bw), lambda bi, h, ti: (0, h)),
                  pl.BlockSpec((1, HG_HEAD_DIM), lambda bi, h, ti: (0, 0))],
        out_specs=pl.BlockSpec((1, tb, bw), lambda bi, h, ti: (bi, ti, h)),
        scratch_shapes=[pltpu.VMEM((heads, HG_HEAD_DIM, HG_HEAD_DIM), F32)],
        compiler_params=_cparams("parallel", "parallel", "arbitrary"),
        name="hgrn2",
    )(proj_hg, proj_hg, proj_hg, proj_hg, lb_param.astype(F32),
      norm_g.reshape(1, HG_HEAD_DIM).astype(F32))


RW_GROUP_HEADS = 4
RW_GROUP = RW_GROUP_HEADS * RW_HEAD_DIM


def _head_ones(n, head):
    r = lax.broadcasted_iota(jnp.int32, (n, n), 0)
    c = lax.broadcasted_iota(jnp.int32, (n, n), 1)
    return jnp.where((r // head) == (c // head), 1.0, 0.0).astype(BF16)


def _head_sums(x, ones_bd):
    n = ones_bd.shape[0]
    rows, width = x.shape
    if width == n:
        return _dot_select_cols(x, ones_bd)
    stacked = jnp.concatenate([x[:, s:s + n] for s in range(0, width, n)], axis=0)
    sums = _dot_select_cols(stacked, ones_bd)
    return jnp.concatenate([sums[g * rows:(g + 1) * rows] for g in range(width // n)], axis=1)


def _softplus(y):
    return jnp.maximum(y, 0.0) + jnp.log(1.0 + jnp.exp(-jnp.abs(y)))


def _rw_prepare_block(i, x_ref, xp_ref, lo_ref, lop_ref, mu_ref, mulo_ref, w0_ref, a0_ref, kk_ref,
                      ka_ref, rk_ref, w2w_ref, w2a_ref, w2g_ref,
                      r_out, lw_out, k_out, v_out, kk_out, kb_out, g_out, bonus_out):
    width = r_out.shape[1]

    def token_shift(cur_ref, prev_ref, mix_ref):
        x = cur_ref[0]
        prev_row = jnp.where(i == 0, 0.0, prev_ref[0, 7:8, :])
        row = lax.broadcasted_iota(jnp.int32, x.shape, 0)
        prev = jnp.where(row == 0, prev_row, pltpu.roll(x, 1, axis=0))
        return x + mix_ref[...] * (prev - x)

    xs = token_shift(x_ref, xp_ref, mu_ref)
    lora = token_shift(lo_ref, lop_ref, mulo_ref)
    w_log = -_softplus(-(w0_ref[...] + _dot(jnp.tanh(lora).astype(BF16), w2w_ref[...]))) - 0.5
    log_decay = -jnp.exp(w_log)
    a = jax.nn.sigmoid(a0_ref[...] + _dot(lora.astype(BF16), w2a_ref[...]))
    g = _dot(jax.nn.sigmoid(lora).astype(BF16), w2g_ref[...])

    r = xs[:, :width]
    k = xs[:, width:2 * width]
    v = xs[:, 2 * width:3 * width]
    ones_bd = _head_ones(RW_GROUP, RW_HEAD_DIM)
    kk = k * kk_ref[...]
    kk = kk * lax.rsqrt(jnp.maximum(_head_sums(kk * kk, ones_bd), 1e-24))
    k = k * (1.0 + (a - 1.0) * ka_ref[...])
    bonus = _head_sums(r * k * rk_ref[...], ones_bd) * v

    r_out[...] = r
    lw_out[...] = log_decay
    k_out[...] = k
    v_out[...] = v
    kk_out[...] = kk
    kb_out[...] = kk * a
    g_out[...] = g
    bonus_out[...] = bonus


def _tile_rows(x, n):
    return jnp.concatenate([x] * n, axis=0)


def _rw_kernel(x_ref, xp_ref, lo_ref, lop_ref, mu_ref, mulo_ref, w0_ref, a0_ref, kkp_ref, ka_ref,
               rk_ref, w2w_ref, w2a_ref, w2g_ref, lnw_ref, lnb_ref, o_ref,
               ht_ref, r_ref, lw_ref, k_ref, v_ref, kk_ref, kb_ref, g_ref, bonus_ref,
               *, chunk):
    nh, hd, gw = RW_GROUP_HEADS, RW_HEAD_DIM, RW_GROUP
    wide = nh * chunk
    tb, width = r_ref.shape
    groups = width // gw
    ti = pl.program_id(1)

    @pl.when(ti == 0)
    def _():
        ht_ref[...] = jnp.zeros_like(ht_ref)

    _rw_prepare_block(ti, x_ref, xp_ref, lo_ref, lop_ref, mu_ref, mulo_ref, w0_ref, a0_ref, kkp_ref,
                      ka_ref, rk_ref, w2w_ref, w2a_ref, w2g_ref,
                      r_ref, lw_ref, k_ref, v_ref, kk_ref, kb_ref, g_ref, bonus_ref)

    def iota(shape, d):
        return lax.broadcasted_iota(jnp.int32, shape, d)

    ltri = jnp.where(iota((chunk, chunk), 1) <= iota((chunk, chunk), 0), 1.0, 0.0).astype(BF16)
    row_cw = iota((chunk, wide), 0)
    pos_cw = iota((chunk, wide), 1) % chunk
    strict = pos_cw < row_cw
    incl = pos_cw <= row_cw
    eye_t = jnp.where(pos_cw == row_cw, 1.0, 0.0)
    off_diag = []
    s = 1
    while s < chunk:
        off_diag.append(((row_cw // (2 * s)) == (pos_cw // (2 * s)))
                        & ((row_cw % (2 * s)) >= s) & ((pos_cw % (2 * s)) < s))
        s *= 2
    bd_wk = (iota((wide, gw), 0) // chunk) == (iota((wide, gw), 1) // hd)
    bd_ww = (iota((wide, wide), 0) // chunk) == (iota((wide, wide), 1) // chunk)
    head_mask = (iota((gw, gw), 0) // hd) == (iota((gw, gw), 1) // hd)
    ones_bd = _head_ones(gw, hd)

    def expand_k(x):
        return jnp.where(bd_wk, _tile_rows(x, nh), 0.0).astype(BF16)

    def expand_w(x):
        return jnp.where(bd_ww, _tile_rows(x, nh), 0.0)

    def mm(a, b):
        return _dot(a.astype(BF16), b.astype(BF16))

    def step(c, _):
        rows = pl.ds(pl.multiple_of(c * chunk, chunk), chunk)
        gs = range(groups)
        lanes = [slice(gi * gw, (gi + 1) * gw) for gi in gs]
        lw = lw_ref[rows, :]
        gam = _dot_select_rows(ltri, lw)
        gam_end = gam[chunk - 1:chunk, :]
        e_neg = jnp.exp(-gam)
        e_suf = jnp.exp(gam_end - gam)
        k_all = k_ref[rows, :]
        kb_all = kb_ref[rows, :]
        v_all = v_ref[rows, :]
        a_t = -kk_ref[rows, :] * jnp.exp(gam - lw)
        r_t = r_ref[rows, :] * jnp.exp(gam)
        b_t = kb_all * e_neg
        k_t = k_all * e_neg
        b_h = kb_all * e_suf
        k_h = k_all * e_suf
        h_decay = jnp.exp(gam_end)

        lhs = [jnp.concatenate([a_t[:, l], r_t[:, l]], axis=0).astype(BF16) for l in lanes]
        sb = [_dot_nt(lhs[g], expand_k(b_t[:, lanes[g]])) for g in gs]
        sk = [_dot_nt(lhs[g], expand_k(k_t[:, lanes[g]])) for g in gs]
        a_ab = [jnp.where(strict, s[:chunk], 0.0) for s in sb]
        a_rb = [jnp.where(incl, s[chunk:], 0.0) for s in sb]
        a_ak = [jnp.where(strict, s[:chunk], 0.0) for s in sk]
        a_rk = [jnp.where(incl, s[chunk:], 0.0) for s in sk]

        hts = [ht_ref[g] for g in gs]
        from_state = [_dot_nt(lhs[g], hts[g].astype(BF16)) for g in gs]
        from_v = [_dot(jnp.concatenate([a_ak[g], a_rk[g]], axis=0).astype(BF16),
                       expand_k(v_all[:, lanes[g]])) for g in gs]

        p_acc = [eye_t + jnp.where(off_diag[0], a, 0.0) for a in a_ab]
        for li in range(1, len(off_diag)):
            left = [mm(p_acc[g], expand_w(jnp.where(off_diag[li], a_ab[g], 0.0))) for g in gs]
            p_acc = [p_acc[g] + mm(left[g], expand_w(p_acc[g])) for g in gs]

        u = [_dot(p_acc[g].astype(BF16), expand_k(from_state[g][:chunk] + from_v[g][:chunk]))
             for g in gs]
        y = [from_state[g][chunk:] + from_v[g][chunk:] + _dot(a_rb[g].astype(BF16), expand_k(u[g]))
             for g in gs]
        upd = [_dot_tn(jnp.concatenate([u[g], v_all[:, lanes[g]]], axis=0).astype(BF16),
                       jnp.concatenate([b_h[:, lanes[g]], k_h[:, lanes[g]]], axis=0).astype(BF16))
               for g in gs]
        for g in gs:
            ht_ref[g] = hts[g] * h_decay[:, lanes[g]] + jnp.where(head_mask, upd[g], 0.0)

        y_all = jnp.concatenate(y, axis=0)
        yc_all = y_all - _head_sums(y_all, ones_bd) * (1.0 / hd)
        var_all = _head_sums(yc_all * yc_all, ones_bd) * (1.0 / hd)
        yn_all = yc_all * lax.rsqrt(var_all + RW_GN_EPS)
        for g in gs:
            l = lanes[g]
            yn = yn_all[g * chunk:(g + 1) * chunk] * lnw_ref[:, l] + lnb_ref[:, l]
            out = (yn + bonus_ref[rows, l]) * g_ref[rows, l]
            o_ref[0, rows, l] = out.astype(o_ref.dtype)
        return 0

    lax.fori_loop(0, tb // chunk, step, 0, unroll=4)


def rwkv7(rkv, lora, mu, w0, w_w2, a0, w_a2, w_g2, k_k, k_a, r_k, lnx_w, lnx_b, *, chunk=64, tb=256):
    b, t, cols = rkv.shape
    width = cols // 3
    tb = min(tb, t)

    def padded(w2, first_row):
        full = jnp.zeros((RW_LORA_COLS, width), F32)
        return lax.dynamic_update_slice(full, w2.astype(F32), (first_row, 0)).astype(BF16)

    w2w = padded(w_w2, 0)
    w2a = padded(w_a2, RW_DECAY_RANK)
    w2g = padded(w_g2, RW_DECAY_RANK + RW_AAA_RANK)

    def vec(p):
        return p.reshape(1, -1).astype(F32)

    def vspec(n):
        return pl.BlockSpec((1, n), lambda bi, i: (0, 0))

    def cur(n):
        return pl.BlockSpec((1, tb, n), lambda bi, i: (bi, i, 0))

    def prev(n):
        return pl.BlockSpec((1, 8, n), lambda bi, i: (bi, jnp.maximum(i * (tb // 8) - 1, 0), 0))

    wspec = pl.BlockSpec((RW_LORA_COLS, width), lambda bi, i: (0, 0))
    return pl.pallas_call(
        functools.partial(_rw_kernel, chunk=chunk),
        out_shape=jax.ShapeDtypeStruct((b, t, width), BF16),
        grid=(b, t // tb),
        in_specs=[cur(cols), prev(cols), cur(RW_LORA_COLS), prev(RW_LORA_COLS),
                  vspec(cols), vspec(RW_LORA_COLS),
                  vspec(width), vspec(width), vspec(width), vspec(width), vspec(width),
                  wspec, wspec, wspec, vspec(width), vspec(width)],
        out_specs=cur(width),
        scratch_shapes=[pltpu.VMEM((width // RW_GROUP, RW_GROUP, RW_GROUP), F32)]
                       + [pltpu.VMEM((tb, width), F32)] * 8,
        compiler_params=_cparams("parallel", "arbitrary"),
        name="rwkv7",
    )(rkv, rkv, lora, lora, vec(mu[:cols]), vec(mu[cols:]), vec(w0), vec(a0), vec(k_k), vec(k_a),
      vec(r_k), w2w, w2a, w2g, vec(lnx_w), vec(lnx_b))


def kernel(x, norm1_g, w_in, sb_norm_g, hg_lb_param, hg_norm_g, rw_mu, rw_w0, rw_w_w2, rw_a0,
           rw_w_a2, rw_w_g2, rw_k_k, rw_k_a, rw_r_k, rw_lnx_w, rw_lnx_b, w_out, norm2_g,
           w_ff_in, w_ff_out, final_g):
    b, t, d = x.shape
    depth = norm1_g.shape[0]
    sb_w = sb_norm_g.shape[1]
    hg_w = hg_lb_param.shape[1]
    rw_w = rw_lnx_w.shape[1]
    sb_cols, hg_cols, rkv_cols = 3 * sb_w, 4 * hg_w, 3 * rw_w
    m = b * t
    xf = x.reshape(m, d)
    w_in_l = w_in[:1].astype(BF16)
    h, ss = rmsnorm(xf, norm1_g[0], BF16), None
    for l in range(depth):
        proj = functools.partial(matmul, h, w_in_l, bm=1024, row_ss=ss)
        p_sb = proj(bn=1024, col0=0, n=sb_cols, out_dtype=BF16, name="proj_sb").reshape(b, t, -1)
        p_hg, w_out_l = proj(bn=1024, col0=sb_cols, n=hg_cols, side_cast=(w_out, l), name="proj_hg")
        p_hg = p_hg.reshape(b, t, -1)
        p_rkv = proj(bn=1024, col0=sb_cols + hg_cols, n=rkv_cols, name="proj_rkv").reshape(b, t, -1)
        p_lora = proj(bn=RW_LORA_COLS, col0=sb_cols + hg_cols + rkv_cols, n=RW_LORA_COLS,
                      name="proj_lora").reshape(b, t, -1)
        o_sb = sb_attention(p_sb, sb_norm_g[l])
        o_hg = hgrn2(p_hg, hg_lb_param, hg_norm_g[l], layer=l)
        o_rw = rwkv7(p_rkv, p_lora, rw_mu[l], rw_w0[l], rw_w_w2[l], rw_a0[l], rw_w_a2[l],
                     rw_w_g2[l], rw_k_k[l], rw_k_a[l], rw_r_k[l], rw_lnx_w[l], rw_lnx_b[l])
        mix = [o.reshape(m, -1) for o in (o_sb, o_hg, o_rw)]
        xf, h, ss, w_ff_in_l = matmul(mix, w_out_l, bm=1024, bn=512, residual=xf,
                                      next_norm_g=norm2_g[l], side_cast=(w_ff_in, l), name="out_proj")
        ff, w_ff_out_l = matmul(h, w_ff_in_l, bm=1024, bn=1024, relu2=True, row_ss=ss, out_dtype=BF16,
                                side_cast=(w_ff_out, l), name="ff_in")
        ff_out = functools.partial(matmul, ff, w_ff_out_l, bm=1024, bn=1024, bk=2048, residual=xf)
        if l + 1 < depth:
            xf, h, ss, w_in_l = ff_out(next_norm_g=norm1_g[l + 1], side_cast=(w_in, l + 1),
                                       name="ff_out_norm")
        else:
            xf = ff_out(name="ff_out")
    return rmsnorm(xf, final_g, F32).reshape(b, t, d)
```

```python
import functools

import jax
import jax.numpy as jnp
from jax import lax
from jax.experimental import pallas as pl
from jax.experimental.pallas import tpu as pltpu

F32 = jnp.float32
BF16 = jnp.bfloat16

NORM_EPS = 1e-5
V7X_VMEM_BYTES = 64 * 1024 * 1024
VMEM_LIMIT = V7X_VMEM_BYTES - 8 * 1024 * 1024
LANES = 128
BF16_SUBLANES = 16

SB_HEAD_DIM = 128
SB_LOG_WEIGHT_CUTOFF = -90.0
HG_HEAD_DIM = 128
HG_MIN_F = 1e-30
RW_HEAD_DIM = 64
RW_GN_EPS = 64e-5
RW_DECAY_RANK = 96
RW_AAA_RANK = 96
RW_GATE_RANK = 64
RW_LORA_COLS = RW_DECAY_RANK + RW_AAA_RANK + RW_GATE_RANK


def _cparams(*sem):
    return pltpu.CompilerParams(dimension_semantics=sem, vmem_limit_bytes=VMEM_LIMIT)


def _dot(a, b):
    return jnp.dot(a, b, preferred_element_type=F32)


def _dot_nt(a, b):
    return lax.dot_general(a, b, (((1,), (1,)), ((), ())), preferred_element_type=F32)


def _dot_tn(a, b):
    return lax.dot_general(a, b, (((0,), (0,)), ((), ())), preferred_element_type=F32)


def _split2(x):
    hi = x.astype(BF16)
    lo = (x - hi.astype(F32)).astype(BF16)
    return hi, lo


def _dot_select_rows(a01, x):
    hi, lo = _split2(x)
    return _dot(a01, hi) + _dot(a01, lo)


def _dot_select_cols(x, b01):
    hi, lo = _split2(x)
    return _dot(hi, b01) + _dot(lo, b01)


def _rmsnorm_kernel(x_ref, g_ref, o_ref):
    x = x_ref[...]
    ms = jnp.mean(x * x, axis=-1, keepdims=True)
    o_ref[...] = (x * lax.rsqrt(ms + NORM_EPS) * g_ref[...]).astype(o_ref.dtype)


def rmsnorm(x, g, out_dtype, bm=512):
    m, d = x.shape
    bm = min(bm, m)
    return pl.pallas_call(
        _rmsnorm_kernel,
        out_shape=jax.ShapeDtypeStruct((m, d), out_dtype),
        grid=(m // bm,),
        in_specs=[pl.BlockSpec((bm, d), lambda i: (i, 0)),
                  pl.BlockSpec((1, d), lambda i: (0, 0))],
        out_specs=pl.BlockSpec((bm, d), lambda i: (i, 0)),
        compiler_params=_cparams("parallel"),
        name="rmsnorm",
    )(x, g.reshape(1, d).astype(F32))


def _matmul_kernel(*refs, n_lhs, nk, relu2, has_res, has_row_ss, emit_norm, has_side, norm_dim):
    refs = list(refs)
    a_refs = [refs.pop(0) for _ in range(n_lhs)]
    b_ref = refs.pop(0)
    r_ref = refs.pop(0) if has_res else None
    ss_in_ref = refs.pop(0) if has_row_ss else None
    gnext_ref = refs.pop(0) if emit_norm else None
    side_in_ref = refs.pop(0) if has_side else None
    o_ref = refs.pop(0)
    xg_ref, ss_out_ref = (refs.pop(0), refs.pop(0)) if emit_norm else (None, None)
    side_out_ref = refs.pop(0) if has_side else None
    rest = refs
    j = pl.program_id(1)

    if has_side:
        side_out_ref[...] = side_in_ref[...].astype(side_out_ref.dtype)

    def product():
        acc, k0 = None, 0
        for a_ref in a_refs:
            kw = a_ref.shape[1]
            part = _dot(a_ref[...], b_ref[k0:k0 + kw, :])
            acc = part if acc is None else acc + part
            k0 += kw
        return acc

    def finish(acc):
        if has_row_ss:
            acc = acc * lax.rsqrt(ss_in_ref[:, :1] * (1.0 / norm_dim) + NORM_EPS)
        if relu2:
            acc = jnp.square(jnp.maximum(acc, 0.0))
        if has_res:
            acc = acc + r_ref[...]
        o_ref[...] = acc.astype(o_ref.dtype)
        if emit_norm:
            xg_ref[...] = (acc * gnext_ref[...]).astype(xg_ref.dtype)
            part = jnp.broadcast_to(jnp.sum(acc * acc, axis=-1, keepdims=True), ss_out_ref.shape)

            @pl.when(j == 0)
            def _():
                ss_out_ref[...] = part

            @pl.when(j > 0)
            def _():
                ss_out_ref[...] += part

    if nk == 1:
        finish(product())
        return

    acc_ref = rest[0]
    k = pl.program_id(2)

    @pl.when(k == 0)
    def _():
        acc_ref[...] = product()

    @pl.when((k > 0) & (k < nk - 1))
    def _():
        acc_ref[...] += product()

    @pl.when(k == nk - 1)
    def _():
        finish(acc_ref[...] + product())


def matmul(a, b, *, bm, bn, bk=None, layer=0, col0=0, n=None, relu2=False, residual=None,
           row_ss=None, next_norm_g=None, side_cast=None, out_dtype=F32, name="matmul"):
    a_list = list(a) if isinstance(a, (list, tuple)) else [a]
    m = a_list[0].shape[0]
    kdim = sum(x.shape[1] for x in a_list)
    b_rows, b_cols = b.shape[-2:]
    n = b_cols - col0 if n is None else n
    bm, bn = min(bm, m), min(bn, n)
    bk = kdim if bk is None else min(bk, kdim)
    assert b_rows == kdim and m % bm == 0 and n % bn == 0 and kdim % bk == 0 and col0 % bn == 0
    nk = kdim // bk
    assert nk == 1 or len(a_list) == 1
    has_res = residual is not None
    jb0 = col0 // bn
    if len(a_list) == 1:
        in_specs = [pl.BlockSpec((bm, bk), lambda i, j, k: (i, k))]
    else:
        in_specs = [pl.BlockSpec((bm, x.shape[1]), lambda i, j, k: (i, 0)) for x in a_list]
    if b.ndim == 3:
        in_specs.append(pl.BlockSpec((None, bk, bn), lambda i, j, k: (layer, k, jb0 + j)))
    else:
        in_specs.append(pl.BlockSpec((bk, bn), lambda i, j, k: (k, jb0 + j)))
    args = a_list + [b]
    if has_res:
        in_specs.append(pl.BlockSpec((bm, bn), lambda i, j, k: (i, j)))
        args.append(residual)
    stat_spec = pl.BlockSpec((bm, LANES), lambda i, j, k: (i, 0))
    if row_ss is not None:
        in_specs.append(stat_spec)
        args.append(row_ss)
    emit_norm = next_norm_g is not None
    out_shape = jax.ShapeDtypeStruct((m, n), out_dtype)
    out_specs = pl.BlockSpec((bm, bn), lambda i, j, k: (i, j))
    if emit_norm:
        assert col0 == 0 and n == b_cols
        in_specs.append(pl.BlockSpec((1, bn), lambda i, j, k: (0, j)))
        args.append(next_norm_g.reshape(1, n).astype(F32))
        out_shape = [out_shape, jax.ShapeDtypeStruct((m, n), BF16),
                     jax.ShapeDtypeStruct((m, LANES), F32)]
        out_specs = [out_specs, pl.BlockSpec((bm, bn), lambda i, j, k: (i, j)), stat_spec]
    else:
        out_shape, out_specs = [out_shape], [out_specs]
    grid = (m // bm, n // bn, nk)
    if side_cast is not None:
        src, src_layer = side_cast
        _, src_rows, src_cols = src.shape
        steps = grid[0] * grid[1] * grid[2]
        side_rows = max(src_rows // steps, BF16_SUBLANES)
        n_slices = src_rows // side_rows
        assert src_rows == n_slices * side_rows and steps % n_slices == 0
        hold = steps // n_slices

        def side_block(i, j, k):
            return ((i * grid[1] + j) * grid[2] + k) // hold

        in_specs.append(pl.BlockSpec((None, side_rows, src_cols),
                                     lambda i, j, k: (src_layer, side_block(i, j, k), 0)))
        args.append(src)
        out_shape.append(jax.ShapeDtypeStruct((src_rows, src_cols), BF16))
        out_specs.append(pl.BlockSpec((side_rows, src_cols), lambda i, j, k: (side_block(i, j, k), 0)))
    outs = pl.pallas_call(
        functools.partial(_matmul_kernel, n_lhs=len(a_list), nk=nk, relu2=relu2, has_res=has_res,
                          has_row_ss=row_ss is not None, emit_norm=emit_norm,
                          has_side=side_cast is not None, norm_dim=kdim),
        out_shape=out_shape,
        grid=grid,
        in_specs=in_specs,
        out_specs=out_specs,
        scratch_shapes=[pltpu.VMEM((bm, bn), F32)] if nk > 1 else [],
        compiler_params=_cparams("parallel", "arbitrary" if emit_norm else "parallel", "arbitrary"),
        name=name,
    )(*args)
    return outs[0] if len(outs) == 1 else tuple(outs)


def _sb_kernel(q_ref, k_ref, v_ref, g_ref, o_ref, *, blk, heads):
    i = pl.program_id(2)
    scale = SB_HEAD_DIM ** -0.5
    hd = SB_HEAD_DIM
    qs = [q_ref[0, :, h * hd:(h + 1) * hd].astype(BF16) for h in range(heads)]
    row = lax.broadcasted_iota(jnp.int32, (blk, blk), 0)
    col = lax.broadcasted_iota(jnp.int32, (blk, blk), 1)
    before = col < row
    r2 = lax.broadcasted_iota(jnp.int32, (blk, 2 * blk), 0)
    c2 = lax.broadcasted_iota(jnp.int32, (blk, 2 * blk), 1)
    later = jnp.where((r2 > c2) | (c2 >= blk), 1.0, 0.0).astype(BF16)

    def blocks(specs, carry):
        hs = range(heads)
        units = [(s, h) for s in range(len(specs)) for h in hs]
        starts = [pl.multiple_of(j * blk, blk) for j, _ in specs]
        kbs = [k_ref[0, pl.ds(starts[s], blk), h * hd:(h + 1) * hd].astype(BF16) for s, h in units]
        vbs = [v_ref[0, pl.ds(starts[s], blk), h * hd:(h + 1) * hd].astype(BF16) for s, h in units]
        zs = [_dot_nt(qs[h], kbs[u]) * scale for u, (s, h) in enumerate(units)]
        log_betas = [jnp.minimum(z, 0.0) - jnp.log(1.0 + jnp.exp(-jnp.abs(z))) for z in zs]
        log_keeps = [lb - z for lb, z in zip(log_betas, zs)]
        log_keeps = [lk if specs[s][1] is None else jnp.where(specs[s][1], lk, 0.0)
                     for lk, (s, h) in zip(log_keeps, units)]
        cs_all = _dot_select_cols(jnp.concatenate(log_keeps, axis=0), later)
        css = [cs_all[u * blk:(u + 1) * blk] for u in range(len(units))]
        accs = [carry[2 * h] for h in hs]
        cums = [carry[2 * h + 1] for h in hs]
        for u, (s, h) in enumerate(units):
            w = jnp.exp(log_betas[u] + cums[h] + css[u][:, :blk])
            if specs[s][1] is not None:
                w = jnp.where(specs[s][1], w, 0.0)
            accs[h] = accs[h] + _dot(w.astype(BF16), vbs[u])
            cums[h] = cums[h] + css[u][:, blk:]
        out = []
        for h in hs:
            out += [accs[h], cums[h]]
        return tuple(out)

    carry = (jnp.zeros((blk, hd), F32), jnp.zeros((blk, blk), F32)) * heads
    has_previous = (jnp.zeros((blk, blk), jnp.int32) + i) > 0
    carry = blocks([(i, before), (jnp.maximum(i - 1, 0), has_previous)], carry)

    def alive(state):
        j = state[0]
        top = state[2]
        for h in range(1, heads):
            top = jnp.maximum(top, state[2 + 2 * h])
        return (j >= 0) & (jnp.max(top) > SB_LOG_WEIGHT_CUTOFF)

    def body(state):
        j = state[0]
        return (j - 1,) + blocks([(j, None)], state[1:])

    state = lax.while_loop(alive, body, (i - 2,) + carry)
    for h in range(heads):
        acc = state[1 + 2 * h]
        ms = jnp.mean(acc * acc, axis=-1, keepdims=True)
        o_ref[0, :, h * hd:(h + 1) * hd] = (
            acc * lax.rsqrt(ms + NORM_EPS) * g_ref[:, h * hd:(h + 1) * hd]).astype(o_ref.dtype)


def sb_attention(proj_sb, norm_g, *, blk=128, heads=8):
    b, t, w3 = proj_sb.shape
    w = w3 // 3
    bw = heads * SB_HEAD_DIM
    nh = w // bw
    blk = min(blk, t)
    return pl.pallas_call(
        functools.partial(_sb_kernel, blk=blk, heads=heads),
        out_shape=jax.ShapeDtypeStruct((b, t, w), BF16),
        grid=(b, nh, t // blk),
        in_specs=[pl.BlockSpec((1, blk, bw), lambda bi, h, i: (bi, i, h)),
                  pl.BlockSpec((1, t, bw), lambda bi, h, i: (bi, 0, nh + h)),
                  pl.BlockSpec((1, t, bw), lambda bi, h, i: (bi, 0, 2 * nh + h)),
                  pl.BlockSpec((1, bw), lambda bi, h, i: (0, h))],
        out_specs=pl.BlockSpec((1, blk, bw), lambda bi, h, i: (bi, i, h)),
        compiler_params=_cparams("parallel", "parallel", "arbitrary"),
        name="sb_attention",
    )(proj_sb, proj_sb, proj_sb, norm_g.reshape(1, w).astype(F32))


def _hg_level_halves(chunk):
    halves = []
    m = chunk // 2
    while m >= 1:
        halves.append(m)
        m //= 2
    return halves


HG_ROW_TILE = 8


def _hg_decay_selectors(chunk):
    t = lax.broadcasted_iota(jnp.int32, (chunk, chunk), 0)
    j = lax.broadcasted_iota(jnp.int32, (chunk, chunk), 1)
    mats = [j <= t]
    for m in _hg_level_halves(chunk):
        if m >= HG_ROW_TILE:
            continue
        mid = (t // (2 * m)) * (2 * m) + m - 1
        upper = (t % (2 * m)) >= m
        mats.append((upper & (j > mid) & (j <= t)) | (~upper & (j > t) & (j <= mid)))
    return jnp.concatenate([jnp.where(mm, 1.0, 0.0).astype(BF16) for mm in mats], axis=0)


def _hg_level_decays(sums, chunk):
    b = sums[0:chunk]
    out, small = [], 0
    for m in _hg_level_halves(chunk):
        if m >= HG_ROW_TILE:
            mids = [jnp.broadcast_to(b[s + m - 1:s + m, :], (2 * m, b.shape[1]))
                    for s in range(0, chunk, 2 * m)]
            b_mid = mids[0] if len(mids) == 1 else jnp.concatenate(mids, axis=0)
            out.append(jnp.exp(-jnp.abs(b - b_mid)))
        else:
            small += 1
            out.append(jnp.exp(sums[small * chunk:(small + 1) * chunk]))
    return out


def _hg_kernel(q_ref, f_ref, i_ref, g_ref, lbp_ref, ng_ref, o_ref, st_ref, *, chunk, layer, heads):
    hd = HG_HEAD_DIM
    tb = q_ref.shape[1]

    @pl.when(pl.program_id(2) == 0)
    def _():
        st_ref[...] = jnp.zeros_like(st_ref)

    halves = _hg_level_halves(chunk)
    sel = _hg_decay_selectors(chunk)

    p = lbp_ref[...]
    e = jnp.exp(p - jnp.max(p, axis=0, keepdims=True))
    probs = e / jnp.sum(e, axis=0, keepdims=True)
    lb = jnp.zeros((1, heads * hd), F32)
    for l in range(1, layer + 1):
        lb = lb + probs[l:l + 1, :]

    trow = lax.broadcasted_iota(jnp.int32, (chunk, heads * hd), 0)
    ts = lax.broadcasted_iota(jnp.int32, (chunk, chunk), 0)
    ss = lax.broadcasted_iota(jnp.int32, (chunk, chunk), 1)
    ng = ng_ref[...]

    def step(c, _):
        start = pl.multiple_of(c * chunk, chunk)
        rows = pl.ds(start, chunk)
        q2 = q_ref[0, rows, :]
        fr = f_ref[0, rows, :]
        v2 = i_ref[0, rows, :]
        gate = g_ref[0, rows, :]
        f = lb + (1.0 - lb) * jax.nn.sigmoid(fr)
        log_f = jnp.log(jnp.maximum(f, HG_MIN_F))
        key2 = (1.0 - lb) * jax.nn.sigmoid(-fr)
        sums = _dot_select_rows(sel, log_f)
        b = sums[0:chunk]
        b_last = b[chunk - 1:chunk, :]
        level_decay = _hg_level_decays(sums, chunk)
        silu = gate * jax.nn.sigmoid(gate)

        hs = range(heads)
        lanes = [slice(h * hd, (h + 1) * hd) for h in hs]
        states = [st_ref[h] for h in hs]
        vbs = [v2[:, l].astype(BF16) for l in lanes]
        scores = [None] * heads
        for li, m in enumerate(halves):
            el = level_decay[li]
            upper = (trow % (2 * m)) >= m
            ql = jnp.where(upper, q2 * el, 0.0).astype(BF16)
            kl = jnp.where(upper, 0.0, key2 * el).astype(BF16)
            for h in hs:
                sl = _dot_nt(ql[:, lanes[h]], kl[:, lanes[h]])
                if 2 * m < chunk:
                    sl = jnp.where((ts // (2 * m)) == (ss // (2 * m)), sl, 0.0)
                scores[h] = sl if scores[h] is None else scores[h] + sl
        q_pre = (q2 * jnp.exp(b)).astype(BF16)
        k_suf = (key2 * jnp.exp(b_last - b)).astype(BF16)
        qk = q2 * key2
        inter = [_dot_nt(q_pre[:, lanes[h]], states[h].astype(BF16)) for h in hs]
        intra = [_dot(scores[h].astype(BF16), vbs[h]) for h in hs]
        upd = [_dot_tn(vbs[h], k_suf[:, lanes[h]]) for h in hs]
        decay_all = jnp.exp(b_last)
        for h in hs:
            l = lanes[h]
            st_ref[h] = states[h] * decay_all[:, l] + upd[h]
            diag = jnp.sum(qk[:, l], axis=-1, keepdims=True)
            o = intra[h] + diag * v2[:, l] + inter[h]
            ms = jnp.mean(o * o, axis=-1, keepdims=True)
            o = o * lax.rsqrt(ms + NORM_EPS) * ng
            o_ref[0, rows, l] = (o * silu[:, l]).astype(o_ref.dtype)
        return 0

    lax.fori_loop(0, tb // chunk, step, 0, unroll=4)


def hgrn2(proj_hg, lb_param, norm_g, *, layer, chunk=128, heads=4, tb=512):
    b, t, w4 = proj_hg.shape
    w = w4 // 4
    bw = heads * HG_HEAD_DIM
    nh = w // bw
    tb = min(tb, t)
    chunk = min(chunk, tb)
    depth = lb_param.shape[0]

    def col(g):
        return pl.BlockSpec((1, tb, bw), lambda bi, h, ti: (bi, ti, g * nh + h))

    return pl.pallas_call(
        functools.partial(_hg_kernel, chunk=chunk, layer=layer, heads=heads),
        out_shape=jax.ShapeDtypeStruct((b, t, w), BF16),
        grid=(b, nh, t // tb),
        in_specs=[col(0), col(1), col(2), col(3),
                  pl.BlockSpec((depth, bw), lambda bi, h, ti: (0, h)),
                  pl.BlockSpec((1, HG_HEAD_DIM), lambda bi, h, ti: (0, 0))],
        out_specs=pl.BlockSpec((1, tb, bw), lambda bi, h, ti: (bi, ti, h)),
        scratch_shapes=[pltpu.VMEM((heads, HG_HEAD_DIM, HG_HEAD_DIM), F32)],
        compiler_params=_cparams("parallel", "parallel", "arbitrary"),
        name="hgrn2",
    )(proj_hg, proj_hg, proj_hg, proj_hg, lb_param.astype(F32),
      norm_g.reshape(1, HG_HEAD_DIM).astype(F32))


RW_GROUP_HEADS = 4
RW_GROUP = RW_GROUP_HEADS * RW_HEAD_DIM
RW_PREP_ROWS = 128


def _head_ones(n, head):
    r = lax.broadcasted_iota(jnp.int32, (n, n), 0)
    c = lax.broadcasted_iota(jnp.int32, (n, n), 1)
    return jnp.where((r // head) == (c // head), 1.0, 0.0).astype(BF16)


def _head_sums(x, ones_bd):
    n = ones_bd.shape[0]
    rows, width = x.shape
    if width == n:
        return _dot_select_cols(x, ones_bd)
    stacked = jnp.concatenate([x[:, s:s + n] for s in range(0, width, n)], axis=0)
    sums = _dot_select_cols(stacked, ones_bd)
    return jnp.concatenate([sums[g * rows:(g + 1) * rows] for g in range(width // n)], axis=1)


def _softplus(y):
    return jnp.maximum(y, 0.0) + jnp.log(1.0 + jnp.exp(-jnp.abs(y)))


def _rw_prepare_block(i, x_ref, xp_ref, lo_ref, lop_ref, mu_ref, mulo_ref, w0_ref, a0_ref, kk_ref,
                      ka_ref, rk_ref, w2w_ref, w2a_ref, w2g_ref,
                      r_out, lw_out, k_out, v_out, kk_out, kb_out, g_out, bonus_out):
    n_rows, width = r_out.shape
    sub = min(RW_PREP_ROWS, n_rows)
    ones_bd = _head_ones(RW_GROUP, RW_HEAD_DIM)

    for r0 in range(0, n_rows, sub):
        rows = slice(r0, r0 + sub)

        def token_shift(cur_ref, prev_ref, mix_ref):
            x = cur_ref[0, rows, :]
            if r0 == 0:
                prev_row = jnp.where(i == 0, 0.0, prev_ref[0, 7:8, :])
            else:
                prev_row = cur_ref[0, r0 - 8:r0, :][7:8]
            row = lax.broadcasted_iota(jnp.int32, x.shape, 0)
            prev = jnp.where(row == 0, prev_row, pltpu.roll(x, 1, axis=0))
            return x + mix_ref[...] * (prev - x)

        xs = token_shift(x_ref, xp_ref, mu_ref)
        lora = token_shift(lo_ref, lop_ref, mulo_ref)
        w_log = -_softplus(-(w0_ref[...] + _dot(jnp.tanh(lora).astype(BF16), w2w_ref[...]))) - 0.5
        log_decay = -jnp.exp(w_log)
        a = jax.nn.sigmoid(a0_ref[...] + _dot(lora.astype(BF16), w2a_ref[...]))
        g = _dot(jax.nn.sigmoid(lora).astype(BF16), w2g_ref[...])

        r = xs[:, :width]
        k = xs[:, width:2 * width]
        v = xs[:, 2 * width:3 * width]
        kk = k * kk_ref[...]
        kk = kk * lax.rsqrt(jnp.maximum(_head_sums(kk * kk, ones_bd), 1e-24))
        k = k * (1.0 + (a - 1.0) * ka_ref[...])
        bonus = _head_sums(r * k * rk_ref[...], ones_bd) * v

        r_out[rows, :] = r
        lw_out[rows, :] = log_decay
        k_out[rows, :] = k
        v_out[rows, :] = v
        kk_out[rows, :] = kk
        kb_out[rows, :] = kk * a
        g_out[rows, :] = g
        bonus_out[rows, :] = bonus


def _tile_rows(x, n):
    return jnp.concatenate([x] * n, axis=0)


def _rw_kernel(x_ref, xp_ref, lo_ref, lop_ref, mu_ref, mulo_ref, w0_ref, a0_ref, kkp_ref, ka_ref,
               rk_ref, w2w_ref, w2a_ref, w2g_ref, lnw_ref, lnb_ref, o_ref,
               ht_ref, r_ref, lw_ref, k_ref, v_ref, kk_ref, kb_ref, g_ref, bonus_ref,
               *, chunk):
    nh, hd, gw = RW_GROUP_HEADS, RW_HEAD_DIM, RW_GROUP
    wide = nh * chunk
    tb, width = r_ref.shape
    groups = width // gw
    ti = pl.program_id(1)

    @pl.when(ti == 0)
    def _():
        ht_ref[...] = jnp.zeros_like(ht_ref)

    _rw_prepare_block(ti, x_ref, xp_ref, lo_ref, lop_ref, mu_ref, mulo_ref, w0_ref, a0_ref, kkp_ref,
                      ka_ref, rk_ref, w2w_ref, w2a_ref, w2g_ref,
                      r_ref, lw_ref, k_ref, v_ref, kk_ref, kb_ref, g_ref, bonus_ref)

    def iota(shape, d):
        return lax.broadcasted_iota(jnp.int32, shape, d)

    ltri = jnp.where(iota((chunk, chunk), 1) <= iota((chunk, chunk), 0), 1.0, 0.0).astype(BF16)
    row_cw = iota((chunk, wide), 0)
    pos_cw = iota((chunk, wide), 1) % chunk
    strict = pos_cw < row_cw
    incl = pos_cw <= row_cw
    eye_t = jnp.where(pos_cw == row_cw, 1.0, 0.0)
    off_diag = []
    s = 1
    while s < chunk:
        off_diag.append(((row_cw // (2 * s)) == (pos_cw // (2 * s)))
                        & ((row_cw % (2 * s)) >= s) & ((pos_cw % (2 * s)) < s))
        s *= 2
    bd_wk = (iota((wide, gw), 0) // chunk) == (iota((wide, gw), 1) // hd)
    bd_ww = (iota((wide, wide), 0) // chunk) == (iota((wide, wide), 1) // chunk)
    head_mask = (iota((gw, gw), 0) // hd) == (iota((gw, gw), 1) // hd)
    ones_bd = _head_ones(gw, hd)

    def expand_k(x):
        return jnp.where(bd_wk, _tile_rows(x, nh), 0.0).astype(BF16)

    def expand_w(x):
        return jnp.where(bd_ww, _tile_rows(x, nh), 0.0)

    def mm(a, b):
        return _dot(a.astype(BF16), b.astype(BF16))

    def step(c, _):
        rows = pl.ds(pl.multiple_of(c * chunk, chunk), chunk)
        gs = range(groups)
        lanes = [slice(gi * gw, (gi + 1) * gw) for gi in gs]
        lw = lw_ref[rows, :]
        gam = _dot_select_rows(ltri, lw)
        gam_end = gam[chunk - 1:chunk, :]
        e_neg = jnp.exp(-gam)
        e_suf = jnp.exp(gam_end - gam)
        k_all = k_ref[rows, :]
        kb_all = kb_ref[rows, :]
        v_all = v_ref[rows, :]
        a_t = -kk_ref[rows, :] * jnp.exp(gam - lw)
        r_t = r_ref[rows, :] * jnp.exp(gam)
        b_t = kb_all * e_neg
        k_t = k_all * e_neg
        b_h = kb_all * e_suf
        k_h = k_all * e_suf
        h_decay = jnp.exp(gam_end)

        lhs = [jnp.concatenate([a_t[:, l], r_t[:, l]], axis=0).astype(BF16) for l in lanes]
        sb = [_dot_nt(lhs[g], expand_k(b_t[:, lanes[g]])) for g in gs]
        sk = [_dot_nt(lhs[g], expand_k(k_t[:, lanes[g]])) for g in gs]
        a_ab = [jnp.where(strict, s[:chunk], 0.0) for s in sb]
        a_rb = [jnp.where(incl, s[chunk:], 0.0) for s in sb]
        a_ak = [jnp.where(strict, s[:chunk], 0.0) for s in sk]
        a_rk = [jnp.where(incl, s[chunk:], 0.0) for s in sk]

        hts = [ht_ref[g] for g in gs]
        from_state = [_dot_nt(lhs[g], hts[g].astype(BF16)) for g in gs]
        from_v = [_dot(jnp.concatenate([a_ak[g], a_rk[g]], axis=0).astype(BF16),
                       expand_k(v_all[:, lanes[g]])) for g in gs]

        p_acc = [eye_t + jnp.where(off_diag[0], a, 0.0) for a in a_ab]
        for li in range(1, len(off_diag)):
            left = [mm(p_acc[g], expand_w(jnp.where(off_diag[li], a_ab[g], 0.0))) for g in gs]
            p_acc = [p_acc[g] + mm(left[g], expand_w(p_acc[g])) for g in gs]

        u = [_dot(p_acc[g].astype(BF16), expand_k(from_state[g][:chunk] + from_v[g][:chunk]))
             for g in gs]
        y = [from_state[g][chunk:] + from_v[g][chunk:] + _dot(a_rb[g].astype(BF16), expand_k(u[g]))
             for g in gs]
        upd = [_dot_tn(jnp.concatenate([u[g], v_all[:, lanes[g]]], axis=0).astype(BF16),
                       jnp.concatenate([b_h[:, lanes[g]], k_h[:, lanes[g]]], axis=0).astype(BF16))
               for g in gs]
        for g in gs:
            ht_ref[g] = hts[g] * h_decay[:, lanes[g]] + jnp.where(head_mask, upd[g], 0.0)

        y_all = jnp.concatenate(y, axis=0)
        yc_all = y_all - _head_sums(y_all, ones_bd) * (1.0 / hd)
        var_all = _head_sums(yc_all * yc_all, ones_bd) * (1.0 / hd)
        yn_all = yc_all * lax.rsqrt(var_all + RW_GN_EPS)
        for g in gs:
            l = lanes[g]
            yn = yn_all[g * chunk:(g + 1) * chunk] * lnw_ref[:, l] + lnb_ref[:, l]
            out = (yn + bonus_ref[rows, l]) * g_ref[rows, l]
            o_ref[0, rows, l] = out.astype(o_ref.dtype)
        return 0

    lax.fori_loop(0, tb // chunk, step, 0, unroll=4)


def rwkv7(rkv, lora, mu, w0, w_w2, a0, w_a2, w_g2, k_k, k_a, r_k, lnx_w, lnx_b, *, chunk=64, tb=256):
    b, t, cols = rkv.shape
    width = cols // 3
    tb = min(tb, t)

    def padded(w2, first_row):
        full = jnp.zeros((RW_LORA_COLS, width), F32)
        return lax.dynamic_update_slice(full, w2.astype(F32), (first_row, 0)).astype(BF16)

    w2w = padded(w_w2, 0)
    w2a = padded(w_a2, RW_DECAY_RANK)
    w2g = padded(w_g2, RW_DECAY_RANK + RW_AAA_RANK)

    def vec(p):
        return p.reshape(1, -1).astype(F32)

    def vspec(n):
        return pl.BlockSpec((1, n), lambda bi, i: (0, 0))

    def cur(n):
        return pl.BlockSpec((1, tb, n), lambda bi, i: (bi, i, 0))

    def prev(n):
        return pl.BlockSpec((1, 8, n), lambda bi, i: (bi, jnp.maximum(i * (tb // 8) - 1, 0), 0))

    wspec = pl.BlockSpec((RW_LORA_COLS, width), lambda bi, i: (0, 0))
    return pl.pallas_call(
        functools.partial(_rw_kernel, chunk=chunk),
        out_shape=jax.ShapeDtypeStruct((b, t, width), BF16),
        grid=(b, t // tb),
        in_specs=[cur(cols), prev(cols), cur(RW_LORA_COLS), prev(RW_LORA_COLS),
                  vspec(cols), vspec(RW_LORA_COLS),
                  vspec(width), vspec(width), vspec(width), vspec(width), vspec(width),
                  wspec, wspec, wspec, vspec(width), vspec(width)],
        out_specs=cur(width),
        scratch_shapes=[pltpu.VMEM((width // RW_GROUP, RW_GROUP, RW_GROUP), F32)]
                       + [pltpu.VMEM((tb, width), F32)] * 8,
        compiler_params=_cparams("parallel", "arbitrary"),
        name="rwkv7",
    )(rkv, rkv, lora, lora, vec(mu[:cols]), vec(mu[cols:]), vec(w0), vec(a0), vec(k_k), vec(k_a),
      vec(r_k), w2w, w2a, w2g, vec(lnx_w), vec(lnx_b))


def kernel(x, norm1_g, w_in, sb_norm_g, hg_lb_param, hg_norm_g, rw_mu, rw_w0, rw_w_w2, rw_a0,
           rw_w_a2, rw_w_g2, rw_k_k, rw_k_a, rw_r_k, rw_lnx_w, rw_lnx_b, w_out, norm2_g,
           w_ff_in, w_ff_out, final_g):
    b, t, d = x.shape
    depth = norm1_g.shape[0]
    sb_w = sb_norm_g.shape[1]
    hg_w = hg_lb_param.shape[1]
    rw_w = rw_lnx_w.shape[1]
    sb_cols, hg_cols, rkv_cols = 3 * sb_w, 4 * hg_w, 3 * rw_w
    m = b * t
    xf = x.reshape(m, d)
    w_in_l = w_in[:1].astype(BF16)
    h, ss = rmsnorm(xf, norm1_g[0], BF16), None
    for l in range(depth):
        proj = functools.partial(matmul, h, w_in_l, bm=1024, row_ss=ss)
        p_sb = proj(bn=1024, col0=0, n=sb_cols, out_dtype=BF16, name="proj_sb").reshape(b, t, -1)
        p_hg, w_out_l = proj(bn=1024, col0=sb_cols, n=hg_cols, side_cast=(w_out, l), name="proj_hg")
        p_hg = p_hg.reshape(b, t, -1)
        p_rkv = proj(bn=1024, col0=sb_cols + hg_cols, n=rkv_cols, name="proj_rkv").reshape(b, t, -1)
        p_lora = proj(bn=RW_LORA_COLS, col0=sb_cols + hg_cols + rkv_cols, n=RW_LORA_COLS,
                      name="proj_lora").reshape(b, t, -1)
        o_sb = sb_attention(p_sb, sb_norm_g[l])
        o_hg = hgrn2(p_hg, hg_lb_param, hg_norm_g[l], layer=l)
        o_rw = rwkv7(p_rkv, p_lora, rw_mu[l], rw_w0[l], rw_w_w2[l], rw_a0[l], rw_w_a2[l],
                     rw_w_g2[l], rw_k_k[l], rw_k_a[l], rw_r_k[l], rw_lnx_w[l], rw_lnx_b[l])
        mix = [o.reshape(m, -1) for o in (o_sb, o_hg, o_rw)]
        xf, h, ss, w_ff_in_l = matmul(mix, w_out_l, bm=1024, bn=512, residual=xf,
                                      next_norm_g=norm2_g[l], side_cast=(w_ff_in, l), name="out_proj")
        ff, w_ff_out_l = matmul(h, w_ff_in_l, bm=1024, bn=1024, relu2=True, row_ss=ss, out_dtype=BF16,
                                side_cast=(w_ff_out, l), name="ff_in")
        ff_out = functools.partial(matmul, ff, w_ff_out_l, bm=1024, bn=1024, bk=2048, residual=xf)
        if l + 1 < depth:
            xf, h, ss, w_in_l = ff_out(next_norm_g=norm1_g[l + 1], side_cast=(w_in, l + 1),
                                       name="ff_out_norm")
        else:
            xf = ff_out(name="ff_out")
    return rmsnorm(xf, final_g, F32).reshape(b, t, d)
```
